```python
import math
import jax
import jax.numpy as jnp
from jax import lax
import numpy as np

D_MODEL = 1024
BATCH = 8
SEQ = 8192
DEPTH = 1

CTX_LEN = 256
GRID_W = 64
EPS = 1e-6

HY_WIDTH = D_MODEL
HY_EMB = 33
HY_BANDS = (HY_EMB - 1) // 2
HY_ORDER = 64
HY_FAST_DECAY = 0.3
HY_SLOW_DECAY = 1.5
HY_TARGET = 1e-2

RET_HEADS = 4
RET_DK = D_MODEL // 8
RET_DV = 2 * RET_DK
RET_CHUNK = 128
ROPE_BASE = 10000.0

PEER_HEADS = 8
PEER_N_KEYS = 128
PEER_N_EXPERTS = PEER_N_KEYS * PEER_N_KEYS
PEER_TOPK = 16
PEER_DK = 128
PEER_BLOCK = 128

HY_COLS = 3 * HY_WIDTH
RQ_COLS = RET_HEADS * RET_DK
RV_COLS = RET_HEADS * RET_DV
PROJ_SPLITS = [HY_COLS,
               HY_COLS + RQ_COLS,
               HY_COLS + 2 * RQ_COLS,
               HY_COLS + 2 * RQ_COLS + RV_COLS,
               HY_COLS + 2 * RQ_COLS + 2 * RV_COLS,
               HY_COLS + 2 * RQ_COLS + 2 * RV_COLS + D_MODEL]
PROJ_WIDTH = HY_COLS + 2 * RQ_COLS + 2 * RV_COLS + 2 * D_MODEL

kernel_name = 'hyena_retention_peer_hybrid_dit'


def rms_norm(x, gain):
    xf = x.astype(jnp.float32)
    y = xf * lax.rsqrt(jnp.mean(xf * xf, axis=-1, keepdims=True) + EPS)
    return (y * gain.astype(jnp.float32)).astype(x.dtype)


def modulate(h, shift, scale):
    return h * (1.0 + scale) + shift


def short_conv3(z, w, b):
    zp = jnp.pad(z, ((0, 0), (1, 1), (0, 0)))
    return zp[:, :-2] * w[0] + zp[:, 1:-1] * w[1] + zp[:, 2:] * w[2] + b


def hyena_filter(length, lp):
    f32 = jnp.float32
    t = jnp.linspace(0.0, 1.0, length, dtype=f32)[:, None]
    bands = jnp.linspace(1e-4, HY_BANDS - 1, HY_BANDS, dtype=f32)[None, :]
    w = (2.0 * math.pi / length) * jnp.arange(length, dtype=f32)[:, None]
    z = jnp.concatenate([t, jnp.cos(bands * w), -jnp.sin(bands * w)], axis=-1)
    freq = lp['hy_sin_freq'].astype(f32)
    h = jnp.sin(freq * (z @ lp['hy_fw1'].astype(f32) + lp['hy_fb1'].astype(f32)))
    h = jnp.sin(freq * (h @ lp['hy_fw2'].astype(f32) + lp['hy_fb2'].astype(f32)))
    h = jnp.sin(freq * (h @ lp['hy_fw3'].astype(f32) + lp['hy_fb3'].astype(f32)))
    h = (h @ lp['hy_fw4'].astype(f32)).reshape(length, 2, HY_WIDTH)
    window = jnp.exp(-t * jnp.abs(lp['hy_deltas'].astype(f32)))
    h = h * window[:, None, :]
    kern = jnp.concatenate([h[:, 0], jnp.zeros((1, HY_WIDTH), f32), h[:0:-1, 1]], axis=0)
    return kern / (jnp.sum(jnp.abs(kern), axis=0, keepdims=True) + EPS)


def hyena_long_conv(u, kern, skip):
    length = u.shape[1]
    uf = u.astype(jnp.float32)
    spec = jnp.fft.rfft(uf, n=2 * length, axis=1) * jnp.fft.rfft(kern, n=2 * length, axis=0)[None]
    y = jnp.fft.irfft(spec, n=2 * length, axis=1)[:, :length]
    return (y + uf * skip.astype(jnp.float32)).astype(u.dtype)


def hyena_branch(z, lp):
    length = z.shape[1]
    z = short_conv3(z, lp['hy_conv_w'], lp['hy_conv_b'])
    x0, x1, v = jnp.split(z, 3, axis=-1)
    kern = hyena_filter(length, lp)
    return hyena_long_conv(v * x1, kern, lp['hy_bias']) * x0


def axial_rotary(x, rows, cols):
    half = RET_DK // 2
    nf = half // 2
    inv = ROPE_BASE ** (-jnp.arange(nf, dtype=jnp.float32) / nf)

    def rot(xp, pos):
        ang = pos[:, None] * inv[None, :]
        cos = jnp.cos(ang)[None, :, None, :].astype(x.dtype)
        sin = jnp.sin(ang)[None, :, None, :].astype(x.dtype)
        x1, x2 = xp[..., :nf], xp[..., nf:]
        return jnp.concatenate([x1 * cos - x2 * sin, x1 * sin + x2 * cos], axis=-1)

    return jnp.concatenate([rot(x[..., :half], rows), rot(x[..., half:], cols)], axis=-1)


def retention_scan(q, k, v, log_gamma, state0):
    f32 = jnp.float32
    bsz, length = q.shape[0], q.shape[1]
    n_chunks = length // RET_CHUNK

    def chunks(a):
        return a.astype(f32).reshape(bsz, n_chunks, RET_CHUNK, RET_HEADS, a.shape[-1]).transpose(1, 0, 3, 2, 4)

    lg = log_gamma.astype(f32)[:, None]
    idx = jnp.arange(RET_CHUNK, dtype=f32)
    rel = idx[:, None] - idx[None, :]
    decay_in = jnp.where(rel >= 0, jnp.exp(lg[:, :, None] * jnp.maximum(rel, 0.0)), 0.0)
    xi = jnp.exp(lg * (idx + 1.0))[:, :, None]
    zeta = jnp.exp(lg * (RET_CHUNK - 1.0 - idx))[:, :, None]
    g_chunk = jnp.exp(lg * RET_CHUNK)[:, :, None]

    def step(state, qkv):
        qc, kc, vc = qkv
        scores = jnp.einsum('bhid,bhjd->bhij', qc, kc) * decay_in
        out = (jnp.einsum('bhij,bhjv->bhiv', scores, vc)
               + jnp.einsum('bhid,bhdv->bhiv', qc, state) * xi)
        state = state * g_chunk + jnp.einsum('bhjd,bhjv->bhdv', kc * zeta, vc)
        return state, out

    state, out = lax.scan(step, state0, (chunks(q), chunks(k), chunks(v)))
    out = out.transpose(1, 0, 3, 2, 4).reshape(bsz, length, RET_HEADS, RET_DV)
    return out, state


def bi_retention(q, k, v, lg_f, lg_b, state_f, state_b):
    o_f, s_f = retention_scan(q, k, v, lg_f, state_f)
    o_b, s_b = retention_scan(jnp.flip(q, 1), jnp.flip(k, 1), jnp.flip(v, 1), lg_b, state_b)
    return o_f + jnp.flip(o_b, 1), s_f, s_b


def context_states(k, v, lg_f, lg_b):
    f32 = jnp.float32
    n = k.shape[1]
    pos = jnp.arange(n, dtype=f32)
    w_f = jnp.exp(lg_f.astype(f32)[:, None] * (n - 1.0 - pos))
    w_b = jnp.exp(lg_b.astype(f32)[:, None] * pos)
    kf, vf = k.astype(f32), v.astype(f32)
    s_f = jnp.einsum('blhd,hl,blhv->bhdv', kf, w_f, vf)
    s_b = jnp.einsum('blhd,hl,blhv->bhdv', kf, w_b, vf)
    return s_f, s_b


def retention_readout(o, g):
    bsz, length = o.shape[0], o.shape[1]
    y = o * lax.rsqrt(jnp.mean(o * o, axis=-1, keepdims=True) + EPS)
    gate = jax.nn.silu(g).reshape(bsz, length, RET_HEADS, RET_DV)
    return (y.astype(g.dtype) * gate).reshape(bsz, length, RET_HEADS * RET_DV)


def token_mixer(h_lat, h_ctx, lp, rows, cols, with_ctx_out):
    f32 = jnp.float32
    bsz = h_lat.shape[0]
    hy_l, q_l, k_l, v_l, g_l, ah_l, ar_l = jnp.split(h_lat @ lp['w_in'], PROJ_SPLITS, axis=-1)
    hy_c, q_c, k_c, v_c, g_c, ah_c, ar_c = jnp.split(h_ctx @ lp['w_in'], PROJ_SPLITS, axis=-1)
    scale = RET_DK ** -0.5

    def heads(a, d):
        return a.reshape(a.shape[0], a.shape[1], RET_HEADS, d)

    lg_f, lg_b = lp['ret_log_decay_f'], lp['ret_log_decay_b']
    k_ch = heads(k_c, RET_DK) * scale
    v_ch = heads(v_c, RET_DV)
    if with_ctx_out:
        zero = jnp.zeros((bsz, RET_HEADS, RET_DK, RET_DV), f32)
        o_c, s_f, s_b = bi_retention(heads(q_c, RET_DK), k_ch, v_ch, lg_f, lg_b, zero, zero)
    else:
        s_f, s_b = context_states(k_ch, v_ch, lg_f, lg_b)

    q_lh = axial_rotary(heads(q_l, RET_DK), rows, cols)
    k_lh = axial_rotary(heads(k_l, RET_DK), rows, cols) * scale
    o_l, _, _ = bi_retention(q_lh, k_lh, heads(v_l, RET_DV), lg_f, lg_b, s_f, s_b)

    def merge(hy_proj, o, g, a_h, a_r):
        y_h = hyena_branch(hy_proj, lp) @ lp['w_hy_out']
        y_r = retention_readout(o, g) @ lp['w_ret_out']
        m = jax.nn.sigmoid(a_h) * y_h + jax.nn.sigmoid(a_r) * y_r
        return m @ lp['w_o']

    out_lat = merge(hy_l, o_l, g_l, ah_l, ar_l)
    out_ctx = merge(hy_c, o_c, g_c, ah_c, ar_c) if with_ctx_out else None
    return out_lat, out_ctx


def peer(h, w_query, sub_keys, expert_u, expert_v):
    bsz, length, d = h.shape
    blocks = h.reshape(-1, PEER_BLOCK, d)

    def block_fn(xb):
        q = (xb @ w_query).astype(jnp.float32).reshape(PEER_BLOCK, PEER_HEADS, 2, PEER_DK // 2)
        s = jnp.einsum('phcd,hcnd->phcn', q, sub_keys.astype(jnp.float32))
        s1, i1 = lax.top_k(s[:, :, 0], PEER_TOPK)
        s2, i2 = lax.top_k(s[:, :, 1], PEER_TOPK)
        cand = (s1[..., :, None] + s2[..., None, :]).reshape(PEER_BLOCK, PEER_HEADS, PEER_TOPK * PEER_TOPK)
        cand_idx = (i1[..., :, None] * PEER_N_KEYS + i2[..., None, :]).reshape(PEER_BLOCK, PEER_HEADS, PEER_TOPK * PEER_TOPK)
        top_s, top_pos = lax.top_k(cand, PEER_TOPK)
        eidx = jnp.take_along_axis(cand_idx, top_pos, axis=-1)
        gates = jax.nn.softmax(top_s, axis=-1).astype(xb.dtype)
        u = jnp.take(expert_u, eidx, axis=0)
        act = jax.nn.gelu(jnp.einsum('phkd,pd->phk', u, xb))
        v = jnp.take(expert_v, eidx, axis=0)
        return jnp.einsum('phk,phkd->pd', gates * act, v)

    return lax.map(block_fn, blocks).reshape(bsz, length, d)


def setup_inputs(seed: int = 0) -> dict:
    key = jax.random.key(seed)
    ks = jax.random.split(key, 32)
    f32 = jnp.float32

    def nrm(k, shape, s):
        return jax.random.normal(k, shape, f32) * s

    max_decay = math.log(HY_TARGET) / HY_FAST_DECAY
    min_decay = math.log(HY_TARGET) / HY_SLOW_DECAY
    ret_base = jnp.log1p(-(2.0 ** (-5.0 - jnp.arange(RET_HEADS, dtype=f32))))
    return {
        'x': nrm(ks[0], (BATCH, SEQ, D_MODEL), 1.0),
        'c': nrm(ks[1], (BATCH, D_MODEL), 1.0),
        'ctx': nrm(ks[2], (BATCH, CTX_LEN, D_MODEL), 1.0),
        'c_ctx': nrm(ks[3], (D_MODEL,), 1.0),
        'w_ada': nrm(ks[4], (DEPTH, D_MODEL, 6 * D_MODEL), 0.5 * D_MODEL ** -0.5),
        'b_ada': nrm(ks[5], (DEPTH, 6 * D_MODEL), 0.01),
        'norm1': 1.0 + nrm(ks[6], (DEPTH, D_MODEL), 0.02),
        'norm2': 1.0 + nrm(ks[7], (DEPTH, D_MODEL), 0.02),
        'w_in': nrm(ks[8], (DEPTH, D_MODEL, PROJ_WIDTH), D_MODEL ** -0.5),
        'hy_conv_w': nrm(ks[9], (DEPTH, 3, HY_COLS), 3 ** -0.5),
        'hy_conv_b': nrm(ks[10], (DEPTH, HY_COLS), 0.01),
        'hy_fw1': nrm(ks[11], (DEPTH, HY_EMB, HY_ORDER), HY_EMB ** -0.5),
        'hy_fb1': nrm(ks[12], (DEPTH, HY_ORDER), 0.1),
        'hy_fw2': nrm(ks[13], (DEPTH, HY_ORDER, HY_ORDER), HY_ORDER ** -0.5),
        'hy_fb2': nrm(ks[14], (DEPTH, HY_ORDER), 0.1),
        'hy_fw3': nrm(ks[15], (DEPTH, HY_ORDER, HY_ORDER), HY_ORDER ** -0.5),
        'hy_fb3': nrm(ks[16], (DEPTH, HY_ORDER), 0.1),
        'hy_fw4': nrm(ks[17], (DEPTH, HY_ORDER, 2 * HY_WIDTH), HY_ORDER ** -0.5),
        'hy_sin_freq': 1.0 + nrm(ks[18], (DEPTH, HY_ORDER), 0.02),
        'hy_deltas': jnp.linspace(min_decay, max_decay, HY_WIDTH, dtype=f32)[None, :] * (1.0 + nrm(ks[19], (DEPTH, HY_WIDTH), 0.02)),
        'hy_bias': nrm(ks[20], (DEPTH, HY_WIDTH), 0.5),
        'ret_log_decay_f': ret_base[None, :] * jnp.exp(nrm(ks[21], (DEPTH, RET_HEADS), 0.1)),
        'ret_log_decay_b': ret_base[None, :] * jnp.exp(nrm(ks[22], (DEPTH, RET_HEADS), 0.1)),
        'w_hy_out': nrm(ks[23], (DEPTH, HY_WIDTH, D_MODEL), HY_WIDTH ** -0.5),
        'w_ret_out': nrm(ks[24], (DEPTH, RV_COLS, D_MODEL), RV_COLS ** -0.5),
        'w_o': nrm(ks[25], (DEPTH, D_MODEL, D_MODEL), D_MODEL ** -0.5),
        'peer_w_query': nrm(ks[26], (DEPTH, D_MODEL, PEER_HEADS * PEER_DK), D_MODEL ** -0.5),
        'peer_sub_keys': nrm(ks[27], (DEPTH, PEER_HEADS, 2, PEER_N_KEYS, PEER_DK // 2), (PEER_DK // 2) ** -0.5),
        'peer_u': nrm(ks[28], (DEPTH, PEER_N_EXPERTS, D_MODEL), D_MODEL ** -0.5),
        'peer_v': nrm(ks[29], (DEPTH, PEER_N_EXPERTS, D_MODEL), PEER_HEADS ** -0.5),
        'final_norm': 1.0 + nrm(ks[30], (D_MODEL,), 0.02),
    }


def reference(x, c, ctx, c_ctx, w_ada, b_ada, norm1, norm2, w_in, hy_conv_w, hy_conv_b,
              hy_fw1, hy_fb1, hy_fw2, hy_fb2, hy_fw3, hy_fb3, hy_fw4, hy_sin_freq, hy_deltas, hy_bias,
              ret_log_decay_f, ret_log_decay_b, w_hy_out, w_ret_out, w_o,
              peer_w_query, peer_sub_keys, peer_u, peer_v, final_norm):
    n_rows = x.shape[1] // GRID_W
    rows = jnp.repeat(jnp.arange(n_rows, dtype=jnp.float32), GRID_W)
    cols = jnp.tile(jnp.arange(GRID_W, dtype=jnp.float32), n_rows)
    x_lat, x_ctx = x, ctx
    for layer in range(DEPTH):
        lp = {
            'w_in': w_in[layer], 'hy_conv_w': hy_conv_w[layer], 'hy_conv_b': hy_conv_b[layer],
            'hy_fw1': hy_fw1[layer], 'hy_fb1': hy_fb1[layer], 'hy_fw2': hy_fw2[layer], 'hy_fb2': hy_fb2[layer],
            'hy_fw3': hy_fw3[layer], 'hy_fb3': hy_fb3[layer], 'hy_fw4': hy_fw4[layer],
            'hy_sin_freq': hy_sin_freq[layer], 'hy_deltas': hy_deltas[layer], 'hy_bias': hy_bias[layer],
            'ret_log_decay_f': ret_log_decay_f[layer], 'ret_log_decay_b': ret_log_decay_b[layer],
            'w_hy_out': w_hy_out[layer], 'w_ret_out': w_ret_out[layer], 'w_o': w_o[layer],
        }
        last = layer == DEPTH - 1
        mod_l = (jax.nn.silu(c) @ w_ada[layer] + b_ada[layer])[:, None, :]
        mod_c = (jax.nn.silu(c_ctx) @ w_ada[layer] + b_ada[layer])[None, None, :]
        sh1, sc1, g1, sh2, sc2, g2 = jnp.split(mod_l, 6, axis=-1)
        csh1, csc1, cg1, csh2, csc2, cg2 = jnp.split(mod_c, 6, axis=-1)

        h_lat = modulate(rms_norm(x_lat, norm1[layer]), sh1, sc1)
        h_ctx = modulate(rms_norm(x_ctx, norm1[layer]), csh1, csc1)
        mix_lat, mix_ctx = token_mixer(h_lat, h_ctx, lp, rows, cols, not last)
        x_lat = x_lat + g1 * mix_lat
        h_lat = modulate(rms_norm(x_lat, norm2[layer]), sh2, sc2)
        x_lat = x_lat + g2 * peer(h_lat, peer_w_query[layer], peer_sub_keys[layer], peer_u[layer], peer_v[layer])
        if not last:
            x_ctx = x_ctx + cg1 * mix_ctx
            h_ctx = modulate(rms_norm(x_ctx, norm2[layer]), csh2, csc2)
            x_ctx = x_ctx + cg2 * peer(h_ctx, peer_w_query[layer], peer_sub_keys[layer], peer_u[layer], peer_v[layer])
    return rms_norm(x_lat, final_norm)
```

```python
import functools
import math

import jax
import jax.numpy as jnp
from jax import lax
from jax.experimental import pallas as pl
from jax.experimental.pallas import tpu as pltpu

F32 = jnp.float32
BF16 = jnp.bfloat16
EPS = 1e-6

GRID_W = 64
RET_HEADS = 4
RET_CHUNK = 128
ROPE_BASE = 10000.0
PEER_HEADS = 8
PEER_N_KEYS = 128
PEER_TOPK = 16
PEER_DK = 128

VMEM_LIMIT_BYTES = 56 * 1024 * 1024
NEG_INF = float("-inf")


def _cparams(*sem):
    return pltpu.CompilerParams(dimension_semantics=sem, vmem_limit_bytes=VMEM_LIMIT_BYTES)


def _bdot(a, b):
    return jnp.dot(a, b, preferred_element_type=F32)


def _bdot_nt(a, b):
    return lax.dot_general(a, b, (((1,), (1,)), ((), ())), preferred_element_type=F32)


def _split(a):
    hi = a.astype(BF16)
    lo = (a - hi.astype(F32)).astype(BF16)
    return hi, lo


def _dot3(a, b):
    ah, al = _split(a)
    bh, bl = _split(b)
    return _bdot(ah, bh) + _bdot(ah, bl) + _bdot(al, bh)


def _sigmoid(x):
    return 1.0 / (1.0 + jnp.exp(-x))


def _silu(x):
    return x * _sigmoid(x)


def _gelu_tanh(x):
    return 0.5 * x * (1.0 + jnp.tanh(math.sqrt(2.0 / math.pi) * (x + 0.044715 * (x * x * x))))


def _rms(x, gain):
    return x * lax.rsqrt(jnp.mean(x * x, axis=-1, keepdims=True) + EPS) * gain


def _const_spec(shape):
    nd = len(shape)
    return pl.BlockSpec(shape, lambda *_: (0,) * nd, pipeline_mode=pl.Buffered(1))


def _mod_kernel(c_ref, w_ref, b_ref, o_ref):
    o_ref[...] = _dot3(_silu(c_ref[...]), w_ref[...]) + b_ref[...]


def _mod(cc, w_ada, b_ada):
    rows, d = cc.shape
    n = w_ada.shape[1]
    return pl.pallas_call(
        _mod_kernel,
        grid=(n // d,),
        in_specs=[pl.BlockSpec((rows, d), lambda j: (0, 0)),
                  pl.BlockSpec((d, d), lambda j: (0, j)),
                  pl.BlockSpec((1, d), lambda j: (0, j))],
        out_specs=pl.BlockSpec((rows, d), lambda j: (0, j)),
        out_shape=jax.ShapeDtypeStruct((rows, n), F32),
        compiler_params=_cparams("arbitrary"),
        name="mod",
    )(cc, w_ada, b_ada)


def _ctx_kernel(ctx_ref, sh_ref, sc_ref, g_ref, wk_ref, wv_ref, lgf_ref, lgb_ref, sf_ref, sb_ref, *, heads, dk, dv):
    n = ctx_ref.shape[0]
    h = (_rms(ctx_ref[...], g_ref[...]) * (1.0 + sc_ref[...]) + sh_ref[...]).astype(BF16)
    kc = _bdot(h, wk_ref[...]) * (dk ** -0.5)
    vc = _bdot(h, wv_ref[...]).astype(BF16)
    pos = lax.broadcasted_iota(jnp.int32, (n, dk), 0).astype(F32)
    for hd in range(heads):
        kh = kc[:, hd * dk:(hd + 1) * dk]
        vh = vc[:, hd * dv:(hd + 1) * dv]
        wf = jnp.exp(lgf_ref[hd:hd + 1, :] * (n - 1.0 - pos))
        wb = jnp.exp(lgb_ref[hd:hd + 1, :] * pos)
        sf_ref[hd] = _bdot((kh * wf).T.astype(BF16), vh)
        sb_ref[hd] = _bdot((kh * wb).T.astype(BF16), vh)


def _ctx_states(ctx, csh, csc, gain, w_in_bf, lgf, lgb, *, k_off, v_off, heads, dk, dv):
    b, n, d = ctx.shape
    kw, vw = heads * dk, heads * dv
    out = jax.ShapeDtypeStruct((b, heads, dk, dv), F32)
    return pl.pallas_call(
        functools.partial(_ctx_kernel, heads=heads, dk=dk, dv=dv),
        grid=(b,),
        in_specs=[pl.BlockSpec((None, n, d), lambda i: (i, 0, 0)),
                  pl.BlockSpec((1, d), lambda i: (0, 0)),
                  pl.BlockSpec((1, d), lambda i: (0, 0)),
                  pl.BlockSpec((1, d), lambda i: (0, 0)),
                  pl.BlockSpec((d, kw), lambda i: (0, k_off // kw)),
                  pl.BlockSpec((d, vw), lambda i: (0, v_off // vw)),
                  pl.BlockSpec((heads, 1), lambda i: (0, 0)),
                  pl.BlockSpec((heads, 1), lambda i: (0, 0))],
        out_specs=[pl.BlockSpec((None, heads, dk, dv), lambda i: (i, 0, 0, 0)),
                   pl.BlockSpec((None, heads, dk, dv), lambda i: (i, 0, 0, 0))],
        out_shape=[out, out],
        compiler_params=_cparams("arbitrary"),
        name="ctx",
    )(ctx, csh, csc, gain, w_in_bf, w_in_bf, lgf, lgb)


def _inproj_kernel(x_ref, sh_ref, sc_ref, g_ref, w_ref, o_ref, *, nc):
    h = (_rms(x_ref[...], g_ref[...]) * (1.0 + sc_ref[...]) + sh_ref[...]).astype(BF16)
    n = w_ref.shape[1]
    for j in range(n // nc):
        o_ref[:, j * nc:(j + 1) * nc] = _bdot(h, w_ref[:, j * nc:(j + 1) * nc]).astype(BF16)


def _inproj(x, sh, sc, gain, w_bf, *, tm):
    b, l, d = x.shape
    n = w_bf.shape[1]
    return pl.pallas_call(
        functools.partial(_inproj_kernel, nc=1024),
        grid=(b, l // tm),
        in_specs=[pl.BlockSpec((None, tm, d), lambda i, j: (i, j, 0)),
                  pl.BlockSpec((None, 1, d), lambda i, j: (i, 0, 0)),
                  pl.BlockSpec((None, 1, d), lambda i, j: (i, 0, 0)),
                  pl.BlockSpec((1, d), lambda i, j: (0, 0)),
                  _const_spec((d, n))],
        out_specs=pl.BlockSpec((None, tm, n), lambda i, j: (i, j, 0)),
        out_shape=jax.ShapeDtypeStruct((b, l, n), BF16),
        compiler_params=_cparams("arbitrary", "arbitrary"),
        name="inproj",
    )(x, sh, sc, gain, w_bf)


HALO = 16


def _hy_pre_kernel(p_ref, pp_ref, pn_ref, w_ref, b_ref, u_ref, x0_ref, *, c, cw):
    i = pl.program_id(1)
    first = i == 0
    last = i == pl.num_programs(1) - 1
    tl = p_ref.shape[0]
    row = lax.broadcasted_iota(jnp.int32, (tl, cw), 0)
    for cb in range(c // cw):
        zs = []
        for part in range(3):
            c0 = part * c + cb * cw
            cur = p_ref[:, c0:c0 + cw].astype(F32)
            prev = pp_ref[:, c0:c0 + cw].astype(F32)[HALO - 1:HALO, :]
            nxt = pn_ref[:, c0:c0 + cw].astype(F32)[0:1, :]
            prev = jnp.where(first, 0.0, prev)
            nxt = jnp.where(last, 0.0, nxt)
            up = jnp.where(row == 0, prev, pltpu.roll(cur, 1, 0))
            dn = jnp.where(row == tl - 1, nxt, pltpu.roll(cur, tl - 1, 0))
            w = w_ref[:, c0:c0 + cw]
            zs.append(up * w[0:1, :] + cur * w[1:2, :] + dn * w[2:3, :] + b_ref[:, c0:c0 + cw])
        x0, x1, v = zs
        u_ref[:, cb * cw:(cb + 1) * cw] = (v * x1).astype(BF16)
        x0_ref[:, cb * cw:(cb + 1) * cw] = x0.astype(BF16)


def _hy_pre(p, conv_w, conv_b, *, c, tl):
    b, l, _ = p.shape
    hb = tl // HALO
    nh = l // HALO
    out = jax.ShapeDtypeStruct((b, l, c), BF16)
    return pl.pallas_call(
        functools.partial(_hy_pre_kernel, c=c, cw=256),
        grid=(b, l // tl),
        in_specs=[pl.BlockSpec((None, tl, 3 * c), lambda i, j: (i, j, 0)),
                  pl.BlockSpec((None, HALO, 3 * c), lambda i, j: (i, jnp.maximum(j * hb - 1, 0), 0)),
                  pl.BlockSpec((None, HALO, 3 * c), lambda i, j: (i, jnp.minimum((j + 1) * hb, nh - 1), 0)),
                  pl.BlockSpec((3, 3 * c), lambda i, j: (0, 0)),
                  pl.BlockSpec((1, 3 * c), lambda i, j: (0, 0))],
        out_specs=[pl.BlockSpec((None, tl, c), lambda i, j: (i, j, 0)),
                   pl.BlockSpec((None, tl, c), lambda i, j: (i, j, 0))],
        out_shape=[out, out],
        compiler_params=_cparams("arbitrary", "arbitrary"),
        name="hy_pre",
    )(p, p, p, conv_w, conv_b)


def _filt_kernel(z_ref, t_ref, w1, b1, w2, b2, w3, b3, w4, fr, dl, hf_ref, hb_ref, as_ref, *, c):
    i = pl.program_id(0)
    tl = z_ref.shape[0]
    f = fr[...]
    h = jnp.sin(f * (_dot3(z_ref[...], w1[...]) + b1[...]))
    h = jnp.sin(f * (_dot3(h, w2[...]) + b2[...]))
    h = jnp.sin(f * (_dot3(h, w3[...]) + b3[...]))
    h4 = _dot3(h, w4[...])
    win = jnp.exp(-t_ref[...] * jnp.abs(dl[...]))
    hf = h4[:, :c] * win
    row = lax.broadcasted_iota(jnp.int32, (tl, c), 0) + i * tl
    hb = jnp.where(row == 0, 0.0, h4[:, c:] * win)

    hf_ref[...] = hf
    hb_ref[...] = hb

    @pl.when(i == 0)
    def _():
        as_ref[...] = jnp.zeros_like(as_ref)

    as_ref[...] += jnp.sum(jnp.abs(hf) + jnp.abs(hb), axis=0, keepdims=True)


def _hy_filter(z, t, w1, b1, w2, b2, w3, b3, w4, fr, dl, *, tl):
    l, e = z.shape
    o = w2.shape[0]
    c = dl.shape[1]
    full = lambda shape: pl.BlockSpec(shape, lambda i: (0, 0))
    return pl.pallas_call(
        functools.partial(_filt_kernel, c=c),
        grid=(l // tl,),
        in_specs=[pl.BlockSpec((tl, e), lambda i: (i, 0)),
                  pl.BlockSpec((tl, 1), lambda i: (i, 0)),
                  full((e, o)), full((1, o)), full((o, o)), full((1, o)), full((o, o)), full((1, o)),
                  full((o, 2 * c)), full((1, o)), full((1, c))],
        out_specs=[pl.BlockSpec((tl, c), lambda i: (i, 0)),
                   pl.BlockSpec((tl, c), lambda i: (i, 0)),
                   pl.BlockSpec((1, c), lambda i: (0, 0))],
        out_shape=[jax.ShapeDtypeStruct((l, c), F32), jax.ShapeDtypeStruct((l, c), F32),
                   jax.ShapeDtypeStruct((1, c), F32)],
        compiler_params=_cparams("arbitrary"),
        name="filt",
    )(z, t, w1, b1, w2, b2, w3, b3, w4, fr, dl)


N2_BLOCK = 8


def _fft_dims(l):
    n = 2 * l
    n1 = math.isqrt(n)
    assert n1 * n1 == n and n1 % 32 == 0, "sequence length must make 2L a square of a multiple of 32"
    return n, n1, n1


def _fft_tables(l, lanes):
    n, n1, n2 = _fft_dims(l)
    h1 = n1 // 2
    k1 = jnp.arange(n1, dtype=jnp.int32)
    th1 = ((k1[:, None] * jnp.arange(h1, dtype=jnp.int32)[None, :]) % n1).astype(F32) * (2.0 * math.pi / n1)
    c1, s1 = jnp.cos(th1), jnp.sin(th1)
    f1f = jnp.concatenate([jnp.concatenate([c1, s1], axis=1), jnp.concatenate([-s1, c1], axis=1)], axis=0)
    f1i = f1f.T
    k2 = jnp.arange(n2, dtype=jnp.int32)
    th2 = ((k2[:, None] * k2[None, :]) % n2).astype(F32) * (2.0 * math.pi / n2)
    c2, s2 = jnp.cos(th2), jnp.sin(th2)
    f2f = jnp.concatenate([jnp.concatenate([c2, s2], axis=1), jnp.concatenate([-s2, c2], axis=1)], axis=0)
    f2i = jnp.concatenate([jnp.concatenate([c2, -s2], axis=1), jnp.concatenate([s2, c2], axis=1)], axis=0)
    tht = (k2[:, None] * k1[None, :]).astype(F32) * (2.0 * math.pi / n)
    twr = jnp.broadcast_to(jnp.cos(tht)[:, :, None], (n2, n1, lanes))
    twi = jnp.broadcast_to(jnp.sin(tht)[:, :, None], (n2, n1, lanes))
    return f1f.astype(BF16), f1i.astype(BF16), f2f.astype(BF16), f2i.astype(BF16), twr, twi


def _fft1_kernel(xr_ref, xi_ref, f_ref, twr_ref, twi_ref, o_ref, *, c):
    n1 = twr_ref.shape[1]
    reps = c // twr_ref.shape[2]
    x = jnp.concatenate([xr_ref[...], xi_ref[...]], axis=0)
    a = _bdot(f_ref[...], x)
    for j in range(twr_ref.shape[0]):
        ar = a[0:n1, j * c:(j + 1) * c]
        ai = a[n1:, j * c:(j + 1) * c]
        wr = jnp.tile(twr_ref[j], (1, reps))
        wi = jnp.tile(twi_ref[j], (1, reps))
        o_ref[0:n1, j * c:(j + 1) * c] = (ar * wr + ai * wi).astype(BF16)
        o_ref[n1:, j * c:(j + 1) * c] = (ai * wr - ar * wi).astype(BF16)


def _fft1(u3, f1f, twr, twi, *, c):
    b, h1, w = u3.shape
    n1 = 2 * h1
    n2 = w // c
    nb = min(N2_BLOCK, n2)
    lanes = twr.shape[2]
    return pl.pallas_call(
        functools.partial(_fft1_kernel, c=c),
        grid=(n2 // nb, b // 2),
        in_specs=[pl.BlockSpec((None, h1, nb * c), lambda j, p: (2 * p, 0, j)),
                  pl.BlockSpec((None, h1, nb * c), lambda j, p: (2 * p + 1, 0, j)),
                  pl.BlockSpec((2 * n1, n1), lambda j, p: (0, 0)),
                  pl.BlockSpec((nb, n1, lanes), lambda j, p: (j, 0, 0)),
                  pl.BlockSpec((nb, n1, lanes), lambda j, p: (j, 0, 0))],
        out_specs=pl.BlockSpec((None, 2 * n1, nb * c), lambda j, p: (p, 0, j)),
        out_shape=jax.ShapeDtypeStruct((b // 2, 2 * n1, w), BF16),
        compiler_params=_cparams("arbitrary", "arbitrary"),
        name="fft1",
    )(u3, u3, f1f, twr, twi)


def _fftk_kernel(pr_ref, pi_ref, qr_ref, qi_ref, f_ref, as_ref, o_ref, *, n):
    p = _bdot(f_ref[...], jnp.concatenate([pr_ref[...], pi_ref[...]], axis=0))
    q = _bdot(f_ref[...], jnp.concatenate([qr_ref[...], qi_ref[...]], axis=0))
    n2 = pr_ref.shape[0]
    scale = 1.0 / (n * (as_ref[...] + EPS))
    o_ref[0:n2, :] = (p[0:n2] + q[0:n2]) * scale
    o_ref[n2:, :] = (p[n2:] - q[n2:]) * scale


def _fft_kernel_spectrum(s1, f2f, asum, *, n):
    _, _, n1, n2, c = s1.shape
    spec = lambda pair, ri: pl.BlockSpec((None, None, None, n2, c), lambda k: (pair, ri, k, 0, 0))
    return pl.pallas_call(
        functools.partial(_fftk_kernel, n=n),
        grid=(n1,),
        in_specs=[spec(0, 0), spec(0, 1), spec(1, 0), spec(1, 1),
                  pl.BlockSpec((2 * n2, 2 * n2), lambda k: (0, 0)),
                  pl.BlockSpec((1, c), lambda k: (0, 0))],
        out_specs=pl.BlockSpec((None, 2 * n2, c), lambda k: (k, 0, 0)),
        out_shape=jax.ShapeDtypeStruct((n1, 2 * n2, c), F32),
        compiler_params=_cparams("arbitrary"),
        name="fftk",
    )(s1, s1, s1, s1, f2f, asum)


def _fft2_kernel(ar_ref, ai_ref, kh_ref, ff_ref, fi_ref, o_ref):
    n2 = ar_ref.shape[0]
    x = _bdot(ff_ref[...], jnp.concatenate([ar_ref[...], ai_ref[...]], axis=0))
    xr, xi = x[0:n2], x[n2:]
    kr, ki = kh_ref[0:n2, :], kh_ref[n2:, :]
    y = jnp.concatenate([xr * kr - xi * ki, xr * ki + xi * kr], axis=0).astype(BF16)
    b = _bdot(fi_ref[...], y)
    o_ref[0] = b[0:n2].astype(BF16)
    o_ref[1] = b[n2:].astype(BF16)


def _fft2(s1, khat, f2f, f2i):
    pairs, _, n1, n2, c = s1.shape
    spec = lambda ri: pl.BlockSpec((None, None, None, n2, c), lambda k, p: (p, ri, k, 0, 0))
    return pl.pallas_call(
        _fft2_kernel,
        grid=(n1, pairs),
        in_specs=[spec(0), spec(1),
                  pl.BlockSpec((None, 2 * n2, c), lambda k, p: (k, 0, 0)),
                  pl.BlockSpec((2 * n2, 2 * n2), lambda k, p: (0, 0)),
                  pl.BlockSpec((2 * n2, 2 * n2), lambda k, p: (0, 0))],
        out_specs=pl.BlockSpec((None, 2, None, n2, c), lambda k, p: (p, 0, k, 0, 0)),
        out_shape=jax.ShapeDtypeStruct((pairs, 2, n1, n2, c), BF16),
        compiler_params=_cparams("arbitrary", "arbitrary"),
        name="fft2",
    )(s1, s1, khat, f2f, f2i)


def _fft3_kernel(b_ref, f_ref, twr_ref, twi_ref, o_ref, t_ref, *, c):
    n1 = twr_ref.shape[1]
    h1 = n1 // 2
    reps = c // twr_ref.shape[2]
    for j in range(twr_ref.shape[0]):
        br = b_ref[0:n1, j * c:(j + 1) * c].astype(F32)
        bi = b_ref[n1:, j * c:(j + 1) * c].astype(F32)
        wr = jnp.tile(twr_ref[j], (1, reps))
        wi = jnp.tile(twi_ref[j], (1, reps))
        t_ref[0:n1, j * c:(j + 1) * c] = (br * wr - bi * wi).astype(BF16)
        t_ref[n1:, j * c:(j + 1) * c] = (bi * wr + br * wi).astype(BF16)
    y = _bdot(f_ref[...], t_ref[...])
    o_ref[0] = y[0:h1].astype(BF16)
    o_ref[1] = y[h1:].astype(BF16)


def _fft3(s2, f1i, twr, twi, *, c):
    pairs, rows, w = s2.shape
    n1 = rows // 2
    h1 = n1 // 2
    n2 = w // c
    nb = min(N2_BLOCK, n2)
    lanes = twr.shape[2]
    return pl.pallas_call(
        functools.partial(_fft3_kernel, c=c),
        grid=(n2 // nb, pairs),
        in_specs=[pl.BlockSpec((None, 2 * n1, nb * c), lambda j, p: (p, 0, j)),
                  pl.BlockSpec((n1, 2 * n1), lambda j, p: (0, 0)),
                  pl.BlockSpec((nb, n1, lanes), lambda j, p: (j, 0, 0)),
                  pl.BlockSpec((nb, n1, lanes), lambda j, p: (j, 0, 0))],
        out_specs=pl.BlockSpec((None, 2, h1, nb * c), lambda j, p: (p, 0, 0, j)),
        out_shape=jax.ShapeDtypeStruct((pairs, 2, h1, w), BF16),
        scratch_shapes=[pltpu.VMEM((2 * n1, nb * c), BF16)],
        compiler_params=_cparams("arbitrary", "arbitrary"),
        name="fft3",
    )(s2, f1i, twr, twi)


def _hyena_long_conv(u, hf, hb, asum):
    b, l, c = u.shape
    n, n1, n2 = _fft_dims(l)
    h1 = n1 // 2
    f1f, f1i, f2f, f2i, twr, twi = _fft_tables(l, 128)
    zeros = jnp.zeros((l, c), BF16)
    taps = jnp.stack([hf.astype(BF16), zeros, hb.astype(BF16), zeros]).reshape(4, h1, n2 * c)
    s1k = _fft1(taps, f1f, twr, twi, c=c).reshape(2, 2, n1, n2, c)
    khat = _fft_kernel_spectrum(s1k, f2f, asum, n=n)
    s1 = _fft1(u.reshape(b, h1, n2 * c), f1f, twr, twi, c=c).reshape(b // 2, 2, n1, n2, c)
    s2 = _fft2(s1, khat, f2f, f2i).reshape(b // 2, 2 * n1, n2 * c)
    return _fft3(s2, f1i, twr, twi, c=c).reshape(b, l, c)


def _ret_kernel(q_ref, k_ref, v_ref, g_ref, cos_ref, sin_ref, pm_ref, lgf_ref, lgb_ref, sf0_ref, sb0_ref,
                o_ref, sf_ref, sb_ref, sbs_ref, *, nb, dk):
    s = pl.program_id(2)
    ch = RET_CHUNK
    nc = k_ref.shape[0] // ch
    lgf = lgf_ref[...]
    lgb = lgb_ref[...]
    ii = lax.broadcasted_iota(jnp.int32, (ch, dk), 0).astype(F32)
    scale = dk ** -0.5

    def rot(x_bf, r0):
        sw = _bdot(x_bf, pm_ref[...])
        return x_bf.astype(F32) * cos_ref[r0:r0 + ch, :] + sw * sin_ref[r0:r0 + ch, :]

    @pl.when(s == 0)
    def _():
        sb_ref[...] = sb0_ref[...]

    @pl.when(s == nb)
    def _():
        sf_ref[...] = sf0_ref[...]

    @pl.when(s < nb)
    def _():
        blk = nb - 1 - s
        zb = jnp.exp(lgb * ii)
        gch = jnp.exp(lgb * float(ch))
        for cc in reversed(range(nc)):
            r0 = cc * ch
            kc = rot(k_ref[r0:r0 + ch, :], r0) * scale
            sbs_ref[blk * nc + cc] = sb_ref[...]
            upd = _bdot((kc * zb).T.astype(BF16), v_ref[r0:r0 + ch, :])
            sb_ref[...] = sb_ref[...] * gch + upd

    @pl.when(s >= nb)
    def _():
        blk = s - nb
        ri = lax.broadcasted_iota(jnp.int32, (ch, ch), 0)
        ci = lax.broadcasted_iota(jnp.int32, (ch, ch), 1)
        rel = (ri - ci).astype(F32)
        dmat = jnp.where(ri > ci, jnp.exp(lgf * jnp.maximum(rel, 0.0)),
                         jnp.where(ri < ci, jnp.exp(lgb * jnp.maximum(-rel, 0.0)), 2.0))
        xf = jnp.exp(lgf * (ii + 1.0))
        xb = jnp.exp(lgb * (float(ch) - ii))
        zf = jnp.exp(lgf * (float(ch) - 1.0 - ii))
        gch = jnp.exp(lgf * float(ch))
        for cc in range(nc):
            r0 = cc * ch
            qc = rot(q_ref[r0:r0 + ch, :], r0)
            kc = rot(k_ref[r0:r0 + ch, :], r0) * scale
            vc = v_ref[r0:r0 + ch, :]
            sc = _bdot_nt(qc.astype(BF16), kc.astype(BF16)) * dmat
            qx = jnp.concatenate([qc * xf, qc * xb], axis=1).astype(BF16)
            st = jnp.concatenate([sf_ref[...], sbs_ref[blk * nc + cc]], axis=0).astype(BF16)
            o = _bdot(sc.astype(BF16), vc) + _bdot(qx, st)
            sf_ref[...] = sf_ref[...] * gch + _bdot((kc * zf).T.astype(BF16), vc)
            y = o * lax.rsqrt(jnp.mean(o * o, axis=-1, keepdims=True) + EPS)
            o_ref[r0:r0 + ch, :] = (y * _silu(g_ref[r0:r0 + ch, :].astype(F32))).astype(BF16)


def _retention(p, cosf, sinf, pm, lgf, lgb, sf0, sb0, *, q_off, k_off, v_off, g_off, heads, dk, dv, tb):
    b, l, _ = p.shape
    nb = l // tb
    fwd = lambda s: jnp.maximum(s - nb, 0)
    both = lambda s: jnp.where(s < nb, nb - 1 - s, s - nb)
    return pl.pallas_call(
        functools.partial(_ret_kernel, nb=nb, dk=dk),
        grid=(b, heads, 2 * nb),
        in_specs=[pl.BlockSpec((None, tb, dk), lambda i, h, s: (i, fwd(s), q_off // dk + h)),
                  pl.BlockSpec((None, tb, dk), lambda i, h, s: (i, both(s), k_off // dk + h)),
                  pl.BlockSpec((None, tb, dv), lambda i, h, s: (i, both(s), v_off // dv + h)),
                  pl.BlockSpec((None, tb, dv), lambda i, h, s: (i, fwd(s), g_off // dv + h)),
                  pl.BlockSpec((tb, dk), lambda i, h, s: (both(s), 0)),
                  pl.BlockSpec((tb, dk), lambda i, h, s: (both(s), 0)),
                  pl.BlockSpec((dk, dk), lambda i, h, s: (0, 0)),
                  pl.BlockSpec((None, 1, 1), lambda i, h, s: (h, 0, 0)),
                  pl.BlockSpec((None, 1, 1), lambda i, h, s: (h, 0, 0)),
                  pl.BlockSpec((None, None, dk, dv), lambda i, h, s: (i, h, 0, 0)),
                  pl.BlockSpec((None, None, dk, dv), lambda i, h, s: (i, h, 0, 0))],
        out_specs=pl.BlockSpec((None, tb, dv), lambda i, h, s: (i, fwd(s), h)),
        out_shape=jax.ShapeDtypeStruct((b, l, heads * dv), BF16),
        scratch_shapes=[pltpu.VMEM((dk, dv), F32), pltpu.VMEM((dk, dv), F32),
                        pltpu.VMEM((l // RET_CHUNK, dk, dv), F32)],
        compiler_params=_cparams("arbitrary", "arbitrary", "arbitrary"),
        name="ret",
    )(p, p, p, p, cosf, sinf, pm, lgf, lgb, sf0, sb0)


def _merge_kernel(y_ref, u_ref, x0_ref, r_ref, ah_ref, ar_ref, x_ref, g1_ref, sh_ref, sc_ref, n2_ref, hb_ref,
                  wh_ref, wr_ref, wo_ref, x2_ref, h2_ref):
    yh = (y_ref[...].astype(F32) + u_ref[...].astype(F32) * hb_ref[...]) * x0_ref[...].astype(F32)
    y_h = _bdot(yh.astype(BF16), wh_ref[...])
    y_r = _bdot(r_ref[...], wr_ref[...])
    m = _sigmoid(ah_ref[...].astype(F32)) * y_h + _sigmoid(ar_ref[...].astype(F32)) * y_r
    x2 = x_ref[...] + g1_ref[...] * _bdot(m.astype(BF16), wo_ref[...])
    x2_ref[...] = x2
    h2_ref[...] = (_rms(x2, n2_ref[...]) * (1.0 + sc_ref[...]) + sh_ref[...]).astype(BF16)


def _merge(y, u, x0, r, p, x, g1, sh2, sc2, norm2, hy_bias, wh, wr, wo, *, ah_off, ar_off, tm):
    b, l, d = x.shape
    c = y.shape[2]
    rw = r.shape[2]
    tile = lambda w, cb=0: pl.BlockSpec((None, tm, w), lambda i, j: (i, j, cb))
    vec = pl.BlockSpec((None, 1, d), lambda i, j: (i, 0, 0))
    return pl.pallas_call(
        _merge_kernel,
        grid=(b, l // tm),
        in_specs=[tile(c), tile(c), tile(c), tile(rw), tile(d, ah_off // d), tile(d, ar_off // d), tile(d),
                  vec, vec, vec,
                  pl.BlockSpec((1, d), lambda i, j: (0, 0)),
                  pl.BlockSpec((1, c), lambda i, j: (0, 0)),
                  _const_spec((c, d)), _const_spec((rw, d)), _const_spec((d, d))],
        out_specs=[tile(d), tile(d)],
        out_shape=[jax.ShapeDtypeStruct((b, l, d), F32), jax.ShapeDtypeStruct((b, l, d), BF16)],
        compiler_params=_cparams("arbitrary", "arbitrary"),
        name="merge",
    )(y, u, x0, r, p, p, x, g1, sh2, sc2, norm2, hy_bias, wh, wr, wo)


def _top_sorted(s, k):
    vals = []
    rank = jnp.full(s.shape, float(k), F32)
    for r in range(k):
        m = jnp.max(s, axis=0, keepdims=True)
        vals.append(m)
        hit = s == m
        rank = jnp.where(hit, float(r), rank)
        s = jnp.where(hit, NEG_INF, s)
    return vals, rank


def _peer_pairs(k):
    return [(i, j) for i in range(k) for j in range(k) if (i + 1) * (j + 1) <= k]


def _peer_kernel(h_ref, x_ref, g2_ref, fn_ref, wq_ref, kd_ref, u_ref, vt_ref, o_ref,
                 cn_ref, e1_ref, rk_ref, e2_ref, q_ref, cand_ref, act_ref, w_ref, acc_ref, *, topk, sub):
    j = pl.program_id(2)
    nk = PEER_N_KEYS
    t = h_ref.shape[0]
    eb = u_ref.shape[0]
    pairs = _peer_pairs(topk)

    @pl.when(j == 0)
    def _():
        q = _bdot(h_ref[...], wq_ref[...]).astype(BF16)
        for hd in range(PEER_HEADS):
            q_ref[hd] = q[:, hd * PEER_DK:(hd + 1) * PEER_DK]
        cand_ref[...] = jnp.full(cand_ref.shape, NEG_INF, F32)

        def head(hd, carry):
            st = _bdot_nt(kd_ref[hd], q_ref[hd])
            s1, s2 = st[0:nk], st[nk:]
            a, rank1 = _top_sorted(s1, topk)
            b, rank2 = _top_sorted(s2, topk)
            for r, (pi, pj) in enumerate(pairs):
                cand_ref[r:r + 1, :] = a[pi] + b[pj]
            cs = cand_ref[...]
            tau = _top_sorted(cs, topk)[0][-1]
            sel = cs >= tau
            mx = a[0] + b[0]
            z = jnp.sum(jnp.where(sel, jnp.exp(cs - mx), 0.0), axis=0, keepdims=True)
            self = jnp.where(sel, 1.0, 0.0)
            cnt1 = jnp.zeros(s1.shape, F32)
            for i in range(topk):
                rws = [r for r, (pi, _) in enumerate(pairs) if pi == i]
                cnt_i = jnp.sum(self[rws[0]:rws[-1] + 1], axis=0, keepdims=True)
                cnt1 = jnp.where(rank1 == float(i), cnt_i, cnt1)
            cn_ref[hd] = cnt1
            e1_ref[hd] = jnp.exp(s1 - a[0])
            rk_ref[hd] = rank2
            e2_ref[hd] = jnp.exp(s2 - b[0]) / z
            return carry

        lax.fori_loop(0, PEER_HEADS, head, 0)
        acc_ref[...] = jnp.zeros_like(acc_ref)

    act_ref[...] = _bdot_nt(u_ref[...], h_ref[...])

    def sub_block(sb, carry):
        for half in range(sub // nk):
            i = j * (eb // nk) + sb * (sub // nk) + half
            r0 = pl.multiple_of(sb * sub + half * nk, nk)
            act = _gelu_tanh(act_ref[pl.ds(r0, nk), :])
            gate = jnp.zeros((nk, t), F32)
            for hd in range(PEER_HEADS):
                cnt = cn_ref[hd, pl.ds(i, 1), :]
                e1 = e1_ref[hd, pl.ds(i, 1), :]
                gate = gate + jnp.where(rk_ref[hd] < cnt, e2_ref[hd], 0.0) * e1
            w_ref[pl.ds(r0, nk), :] = (gate * act).astype(BF16)
        return carry

    lax.fori_loop(0, eb // sub, sub_block, 0)
    acc_ref[...] += _bdot(vt_ref[...], w_ref[...])

    @pl.when(j == pl.num_programs(2) - 1)
    def _():
        x3 = x_ref[...] + g2_ref[...] * acc_ref[...].T
        o_ref[...] = _rms(x3, fn_ref[...])


def _peer(h2, x2, g2, fnorm, wq, kd, u_bf, vt_bf, *, t, eb):
    b, l, d = x2.shape
    e = u_bf.shape[0]
    nk = PEER_N_KEYS
    npair = -(-len(_peer_pairs(PEER_TOPK)) // 8) * 8
    tile = pl.BlockSpec((None, t, d), lambda i, m, j: (i, m, 0))
    return pl.pallas_call(
        functools.partial(_peer_kernel, topk=PEER_TOPK, sub=2 * nk),
        grid=(b, l // t, e // eb),
        in_specs=[tile, tile,
                  pl.BlockSpec((None, 1, d), lambda i, m, j: (i, 0, 0)),
                  pl.BlockSpec((1, d), lambda i, m, j: (0, 0)),
                  _const_spec((d, PEER_HEADS * PEER_DK)),
                  _const_spec((PEER_HEADS, 2 * nk, PEER_DK)),
                  pl.BlockSpec((eb, d), lambda i, m, j: (j, 0)),
                  pl.BlockSpec((d, eb), lambda i, m, j: (0, j))],
        out_specs=tile,
        out_shape=jax.ShapeDtypeStruct((b, l, d), F32),
        scratch_shapes=[pltpu.VMEM((PEER_HEADS, nk, t), F32)] * 4
                       + [pltpu.VMEM((PEER_HEADS, t, PEER_DK), BF16),
                          pltpu.VMEM((npair, t), F32),
                          pltpu.VMEM((eb, t), F32),
                          pltpu.VMEM((eb, t), BF16),
                          pltpu.VMEM((d, t), F32)],
        compiler_params=_cparams("arbitrary", "arbitrary", "arbitrary"),
        name="peer",
    )(h2, x2, g2, fnorm, wq, kd, u_bf, vt_bf)


def _filter_features(l, emb):
    bands_n = (emb - 1) // 2
    t = jnp.linspace(0.0, 1.0, l, dtype=F32)[:, None]
    bands = jnp.linspace(1e-4, bands_n - 1, bands_n, dtype=F32)[None, :]
    w = (2.0 * math.pi / l) * jnp.arange(l, dtype=F32)[:, None]
    z = jnp.concatenate([t, jnp.cos(bands * w), -jnp.sin(bands * w)], axis=-1)
    return t, z


def _rotary_tables(l, dk):
    half = dk // 2
    nf = half // 2
    inv = ROPE_BASE ** (-jnp.arange(nf, dtype=F32) / nf)
    pos = jnp.arange(l, dtype=jnp.int32)
    rows = (pos // GRID_W).astype(F32)
    cols = (pos % GRID_W).astype(F32)
    ar = rows[:, None] * inv[None, :]
    ac = cols[:, None] * inv[None, :]
    cosf = jnp.concatenate([jnp.cos(ar), jnp.cos(ar), jnp.cos(ac), jnp.cos(ac)], axis=1)
    sinf = jnp.concatenate([-jnp.sin(ar), jnp.sin(ar), -jnp.sin(ac), jnp.sin(ac)], axis=1)
    lane = jnp.arange(dk, dtype=jnp.int32)
    pm = (lane[:, None] == (lane[None, :] ^ nf)).astype(BF16)
    return cosf, sinf, pm


def kernel(x, c, ctx, c_ctx, w_ada, b_ada, norm1, norm2, w_in, hy_conv_w, hy_conv_b, hy_fw1, hy_fb1, hy_fw2, hy_fb2, hy_fw3, hy_fb3, hy_fw4, hy_sin_freq, hy_deltas, hy_bias, ret_log_decay_f, ret_log_decay_b, w_hy_out, w_ret_out, w_o, peer_w_query, peer_sub_keys, peer_u, peer_v, final_norm):
    assert w_ada.shape[0] == 1, "single-layer configuration"
    b, l, d = x.shape
    assert b % 2 == 0 and l % RET_CHUNK == 0
    heads = RET_HEADS
    dk = d // 8
    dv = 2 * dk
    c_hy = d
    hy_cols = 3 * c_hy
    q_off = hy_cols
    k_off = q_off + heads * dk
    v_off = k_off + heads * dk
    g_off = v_off + heads * dv
    ah_off = g_off + heads * dv
    ar_off = ah_off + d

    rows = -(-(b + 1) // 8) * 8
    cc = jnp.zeros((rows, d), F32).at[:b].set(c).at[b].set(c_ctx)
    mod = _mod(cc, w_ada[0], b_ada[0][None, :])
    mod_l = mod[:b].reshape(b, 6, 1, d)
    sh1, sc1, g1, sh2, sc2, g2 = (mod_l[:, i] for i in range(6))
    mod_c = mod[b].reshape(6, 1, d)
    csh1, csc1 = mod_c[0], mod_c[1]

    w_in_bf = w_in[0].astype(BF16)
    n1g = norm1[0][None, :]
    lgf = ret_log_decay_f[0]
    lgb = ret_log_decay_b[0]

    sf0, sb0 = _ctx_states(ctx, csh1, csc1, n1g, w_in_bf, lgf[:, None], lgb[:, None],
                           k_off=k_off, v_off=v_off, heads=heads, dk=dk, dv=dv)

    p = _inproj(x, sh1, sc1, n1g, w_in_bf, tm=min(512, l))

    u, x0c = _hy_pre(p, hy_conv_w[0], hy_conv_b[0][None, :], c=c_hy, tl=min(256, l))
    emb = hy_fw1.shape[1]
    order = hy_fw1.shape[2]
    t_lin, z = _filter_features(l, emb)
    epad = -(-emb // 64) * 64
    z = jnp.pad(z, ((0, 0), (0, epad - emb)))
    fw1 = jnp.pad(hy_fw1[0], ((0, epad - emb), (0, 0)))
    hf, hb, asum = _hy_filter(z, t_lin, fw1, hy_fb1[0][None, :], hy_fw2[0], hy_fb2[0][None, :],
                              hy_fw3[0], hy_fb3[0][None, :], hy_fw4[0], hy_sin_freq[0][None, :],
                              hy_deltas[0][None, :], tl=min(1024, l))
    del order
    y = _hyena_long_conv(u, hf, hb, asum)

    cosf, sinf, pm = _rotary_tables(l, dk)
    r = _retention(p, cosf, sinf, pm, lgf[:, None, None], lgb[:, None, None], sf0, sb0,
                   q_off=q_off, k_off=k_off, v_off=v_off, g_off=g_off, heads=heads, dk=dk, dv=dv,
                   tb=min(1024, l))

    x2, h2 = _merge(y, u, x0c, r, p, x, g1, sh2, sc2, norm2[0][None, :], hy_bias[0][None, :],
                    w_hy_out[0].astype(BF16), w_ret_out[0].astype(BF16), w_o[0].astype(BF16),
                    ah_off=ah_off, ar_off=ar_off, tm=min(512, l))

    sk = peer_sub_keys[0]
    hk = sk.shape[3]
    zk = jnp.zeros_like(sk[:, 0])
    kd = jnp.concatenate([jnp.concatenate([sk[:, 0], zk], axis=2),
                          jnp.concatenate([zk, sk[:, 1]], axis=2)], axis=1).astype(BF16)
    del hk
    out = _peer(h2, x2, g2, final_norm[None, :], peer_w_query[0].astype(BF16), kd,
                peer_u[0].astype(BF16), peer_v[0].T.astype(BF16), t=min(512, l), eb=2048)
    return out
```

```python
import functools
import math

import jax
import jax.numpy as jnp
from jax import lax
from jax.experimental import pallas as pl
from jax.experimental.pallas import tpu as pltpu

F32 = jnp.float32
BF16 = jnp.bfloat16
EPS = 1e-6

GRID_W = 64
RET_HEADS = 4
RET_CHUNK = 128
ROPE_BASE = 10000.0
PEER_HEADS = 8
PEER_N_KEYS = 128
PEER_TOPK = 16
PEER_DK = 128

VMEM_LIMIT_BYTES = 56 * 1024 * 1024
LANES = 128
MXU = 256
NEG_INF = float("-inf")


def _cparams(*sem):
    return pltpu.CompilerParams(dimension_semantics=sem, vmem_limit_bytes=VMEM_LIMIT_BYTES)


def _bdot(a, b):
    return jnp.dot(a, b, preferred_element_type=F32)


def _bdot_nt(a, b):
    return lax.dot_general(a, b, (((1,), (1,)), ((), ())), preferred_element_type=F32)


def _split(a):
    hi = a.astype(BF16)
    lo = (a - hi.astype(F32)).astype(BF16)
    return hi, lo


def _dot3(a, b):
    ah, al = _split(a)
    bh, bl = _split(b)
    return _bdot(ah, bh) + _bdot(ah, bl) + _bdot(al, bh)


def _sigmoid(x):
    return 1.0 / (1.0 + jnp.exp(-x))


def _silu(x):
    return x * _sigmoid(x)


def _gelu_tanh(x):
    k = 2.0 * math.sqrt(2.0 / math.pi) * math.log2(math.e)
    return x / (1.0 + jnp.exp2(-(x * (k + (k * 0.044715) * (x * x)))))


def _rms(x, gain):
    return x * lax.rsqrt(jnp.mean(x * x, axis=-1, keepdims=True) + EPS) * gain


def _const_spec(shape):
    nd = len(shape)
    return pl.BlockSpec(shape, lambda *_: (0,) * nd, pipeline_mode=pl.Buffered(1))


def _mod_kernel(c_ref, w_ref, b_ref, o_ref):
    o_ref[...] = _dot3(_silu(c_ref[...]), w_ref[...]) + b_ref[...]


def _mod(cc, w_ada, b_ada):
    rows, d = cc.shape
    n = w_ada.shape[1]
    return pl.pallas_call(
        _mod_kernel,
        grid=(n // d,),
        in_specs=[pl.BlockSpec((rows, d), lambda j: (0, 0)),
                  pl.BlockSpec((d, d), lambda j: (0, j)),
                  pl.BlockSpec((1, d), lambda j: (0, j))],
        out_specs=pl.BlockSpec((rows, d), lambda j: (0, j)),
        out_shape=jax.ShapeDtypeStruct((rows, n), F32),
        compiler_params=_cparams("arbitrary"),
        name="mod",
    )(cc, w_ada, b_ada)


def _ctx_kernel(ctx_ref, sh_ref, sc_ref, g_ref, wk_ref, wv_ref, lgf_ref, lgb_ref, sf_ref, sb_ref, *, heads, dk, dv):
    n = ctx_ref.shape[0]
    h = (_rms(ctx_ref[...], g_ref[...]) * (1.0 + sc_ref[...]) + sh_ref[...]).astype(BF16)
    kc = _bdot(h, wk_ref[...]) * (dk ** -0.5)
    vc = _bdot(h, wv_ref[...]).astype(BF16)
    pos = lax.broadcasted_iota(jnp.int32, (n, dk), 0).astype(F32)
    for hd in range(heads):
        kh = kc[:, hd * dk:(hd + 1) * dk]
        vh = vc[:, hd * dv:(hd + 1) * dv]
        wf = jnp.exp(lgf_ref[hd:hd + 1, :] * (n - 1.0 - pos))
        wb = jnp.exp(lgb_ref[hd:hd + 1, :] * pos)
        sf_ref[hd] = _bdot((kh * wf).T.astype(BF16), vh)
        sb_ref[hd] = _bdot((kh * wb).T.astype(BF16), vh)


def _ctx_states(ctx, csh, csc, gain, w_in_bf, lgf, lgb, *, k_off, v_off, heads, dk, dv):
    b, n, d = ctx.shape
    kw, vw = heads * dk, heads * dv
    out = jax.ShapeDtypeStruct((b, heads, dk, dv), F32)
    return pl.pallas_call(
        functools.partial(_ctx_kernel, heads=heads, dk=dk, dv=dv),
        grid=(b,),
        in_specs=[pl.BlockSpec((None, n, d), lambda i: (i, 0, 0)),
                  pl.BlockSpec((1, d), lambda i: (0, 0)),
                  pl.BlockSpec((1, d), lambda i: (0, 0)),
                  pl.BlockSpec((1, d), lambda i: (0, 0)),
                  pl.BlockSpec((d, kw), lambda i: (0, k_off // kw)),
                  pl.BlockSpec((d, vw), lambda i: (0, v_off // vw)),
                  pl.BlockSpec((heads, 1), lambda i: (0, 0)),
                  pl.BlockSpec((heads, 1), lambda i: (0, 0))],
        out_specs=[pl.BlockSpec((None, heads, dk, dv), lambda i: (i, 0, 0, 0)),
                   pl.BlockSpec((None, heads, dk, dv), lambda i: (i, 0, 0, 0))],
        out_shape=[out, out],
        compiler_params=_cparams("arbitrary"),
        name="ctx",
    )(ctx, csh, csc, gain, w_in_bf, w_in_bf, lgf, lgb)


def _inproj_kernel(x_ref, sh_ref, sc_ref, g_ref, w_ref, o_ref, *, nc):
    h = (_rms(x_ref[...], g_ref[...]) * (1.0 + sc_ref[...]) + sh_ref[...]).astype(BF16)
    n = w_ref.shape[1]
    for j in range(n // nc):
        o_ref[:, j * nc:(j + 1) * nc] = _bdot(h, w_ref[:, j * nc:(j + 1) * nc]).astype(BF16)


def _inproj(x, sh, sc, gain, w_bf, *, tm):
    b, l, d = x.shape
    n = w_bf.shape[1]
    return pl.pallas_call(
        functools.partial(_inproj_kernel, nc=1024),
        grid=(b, l // tm),
        in_specs=[pl.BlockSpec((None, tm, d), lambda i, j: (i, j, 0)),
                  pl.BlockSpec((None, 1, d), lambda i, j: (i, 0, 0)),
                  pl.BlockSpec((None, 1, d), lambda i, j: (i, 0, 0)),
                  pl.BlockSpec((1, d), lambda i, j: (0, 0)),
                  _const_spec((d, n))],
        out_specs=pl.BlockSpec((None, tm, n), lambda i, j: (i, j, 0)),
        out_shape=jax.ShapeDtypeStruct((b, l, n), BF16),
        compiler_params=_cparams("arbitrary", "arbitrary"),
        name="inproj",
    )(x, sh, sc, gain, w_bf)


HALO = 16


def _hy_pre_kernel(p_ref, pp_ref, pn_ref, w_ref, b_ref, u_ref, x0_ref, *, c, cw):
    i = pl.program_id(1)
    first = i == 0
    last = i == pl.num_programs(1) - 1
    tl = p_ref.shape[0]
    row = lax.broadcasted_iota(jnp.int32, (tl, cw), 0)
    for cb in range(c // cw):
        zs = []
        for part in range(3):
            c0 = part * c + cb * cw
            cur = p_ref[:, c0:c0 + cw].astype(F32)
            prev = pp_ref[:, c0:c0 + cw].astype(F32)[HALO - 1:HALO, :]
            nxt = pn_ref[:, c0:c0 + cw].astype(F32)[0:1, :]
            prev = jnp.where(first, 0.0, prev)
            nxt = jnp.where(last, 0.0, nxt)
            up = jnp.where(row == 0, prev, pltpu.roll(cur, 1, 0))
            dn = jnp.where(row == tl - 1, nxt, pltpu.roll(cur, tl - 1, 0))
            w = w_ref[:, c0:c0 + cw]
            zs.append(up * w[0:1, :] + cur * w[1:2, :] + dn * w[2:3, :] + b_ref[:, c0:c0 + cw])
        x0, x1, v = zs
        u_ref[:, cb * cw:(cb + 1) * cw] = (v * x1).astype(BF16)
        x0_ref[:, cb * cw:(cb + 1) * cw] = x0.astype(BF16)


def _hy_pre(p, conv_w, conv_b, *, c, tl):
    b, l, _ = p.shape
    hb = tl // HALO
    nh = l // HALO
    out = jax.ShapeDtypeStruct((b, l, c), BF16)
    return pl.pallas_call(
        functools.partial(_hy_pre_kernel, c=c, cw=256),
        grid=(b, l // tl),
        in_specs=[pl.BlockSpec((None, tl, 3 * c), lambda i, j: (i, j, 0)),
                  pl.BlockSpec((None, HALO, 3 * c), lambda i, j: (i, jnp.maximum(j * hb - 1, 0), 0)),
                  pl.BlockSpec((None, HALO, 3 * c), lambda i, j: (i, jnp.minimum((j + 1) * hb, nh - 1), 0)),
                  pl.BlockSpec((3, 3 * c), lambda i, j: (0, 0)),
                  pl.BlockSpec((1, 3 * c), lambda i, j: (0, 0))],
        out_specs=[pl.BlockSpec((None, tl, c), lambda i, j: (i, j, 0)),
                   pl.BlockSpec((None, tl, c), lambda i, j: (i, j, 0))],
        out_shape=[out, out],
        compiler_params=_cparams("arbitrary", "arbitrary"),
        name="hy_pre",
    )(p, p, p, conv_w, conv_b)


def _filt_kernel(z_ref, t_ref, w1, b1, w2, b2, w3, b3, w4, fr, dl, hf_ref, hb_ref, as_ref, *, c):
    i = pl.program_id(0)
    tl = z_ref.shape[0]
    f = fr[...]
    h = jnp.sin(f * (_dot3(z_ref[...], w1[...]) + b1[...]))
    h = jnp.sin(f * (_dot3(h, w2[...]) + b2[...]))
    h = jnp.sin(f * (_dot3(h, w3[...]) + b3[...]))
    h4 = _dot3(h, w4[...])
    win = jnp.exp(-t_ref[...] * jnp.abs(dl[...]))
    hf = h4[:, :c] * win
    row = lax.broadcasted_iota(jnp.int32, (tl, c), 0) + i * tl
    hb = jnp.where(row == 0, 0.0, h4[:, c:] * win)

    hf_ref[...] = hf
    hb_ref[...] = hb

    @pl.when(i == 0)
    def _():
        as_ref[...] = jnp.zeros_like(as_ref)

    as_ref[...] += jnp.sum(jnp.abs(hf) + jnp.abs(hb), axis=0, keepdims=True)


def _hy_filter(z, t, w1, b1, w2, b2, w3, b3, w4, fr, dl, *, tl):
    l, e = z.shape
    o = w2.shape[0]
    c = dl.shape[1]
    full = lambda shape: pl.BlockSpec(shape, lambda i: (0, 0))
    return pl.pallas_call(
        functools.partial(_filt_kernel, c=c),
        grid=(l // tl,),
        in_specs=[pl.BlockSpec((tl, e), lambda i: (i, 0)),
                  pl.BlockSpec((tl, 1), lambda i: (i, 0)),
                  full((e, o)), full((1, o)), full((o, o)), full((1, o)), full((o, o)), full((1, o)),
                  full((o, 2 * c)), full((1, o)), full((1, c))],
        out_specs=[pl.BlockSpec((tl, c), lambda i: (i, 0)),
                   pl.BlockSpec((tl, c), lambda i: (i, 0)),
                   pl.BlockSpec((1, c), lambda i: (0, 0))],
        out_shape=[jax.ShapeDtypeStruct((l, c), F32), jax.ShapeDtypeStruct((l, c), F32),
                   jax.ShapeDtypeStruct((1, c), F32)],
        compiler_params=_cparams("arbitrary"),
        name="filt",
    )(z, t, w1, b1, w2, b2, w3, b3, w4, fr, dl)


N2_BLOCK = 8


def _fft_dims(l):
    n = 2 * l
    n1 = math.isqrt(n)
    assert n1 * n1 == n and n1 % 32 == 0, "sequence length must make 2L a square of a multiple of 32"
    return n, n1, n1


def _fft_tables(l, lanes):
    n, n1, n2 = _fft_dims(l)
    h1 = n1 // 2
    k1 = jnp.arange(n1, dtype=jnp.int32)
    th1 = ((k1[:, None] * jnp.arange(h1, dtype=jnp.int32)[None, :]) % n1).astype(F32) * (2.0 * math.pi / n1)
    c1, s1 = jnp.cos(th1), jnp.sin(th1)
    f1f = jnp.concatenate([jnp.concatenate([c1, s1], axis=1), jnp.concatenate([-s1, c1], axis=1)], axis=0)
    f1i = f1f.T
    k2 = jnp.arange(n2, dtype=jnp.int32)
    th2 = ((k2[:, None] * k2[None, :]) % n2).astype(F32) * (2.0 * math.pi / n2)
    c2, s2 = jnp.cos(th2), jnp.sin(th2)
    f2f = jnp.concatenate([jnp.concatenate([c2, s2], axis=1), jnp.concatenate([-s2, c2], axis=1)], axis=0)
    f2i = jnp.concatenate([jnp.concatenate([c2, -s2], axis=1), jnp.concatenate([s2, c2], axis=1)], axis=0)
    tht = (k2[:, None] * k1[None, :]).astype(F32) * (2.0 * math.pi / n)
    twr = jnp.broadcast_to(jnp.cos(tht)[:, :, None], (n2, n1, lanes))
    twi = jnp.broadcast_to(jnp.sin(tht)[:, :, None], (n2, n1, lanes))
    return f1f.astype(BF16), f1i.astype(BF16), f2f.astype(BF16), f2i.astype(BF16), twr, twi


def _fft1_kernel(xr_ref, xi_ref, f_ref, twr_ref, twi_ref, o_ref, *, c):
    n1 = twr_ref.shape[1]
    reps = c // twr_ref.shape[2]
    x = jnp.concatenate([xr_ref[...], xi_ref[...]], axis=0)
    a = _bdot(f_ref[...], x)
    for j in range(twr_ref.shape[0]):
        ar = a[0:n1, j * c:(j + 1) * c]
        ai = a[n1:, j * c:(j + 1) * c]
        wr = jnp.tile(twr_ref[j], (1, reps))
        wi = jnp.tile(twi_ref[j], (1, reps))
        o_ref[0:n1, j * c:(j + 1) * c] = (ar * wr + ai * wi).astype(BF16)
        o_ref[n1:, j * c:(j + 1) * c] = (ai * wr - ar * wi).astype(BF16)


def _fft1(u3, f1f, twr, twi, *, c):
    b, h1, w = u3.shape
    n1 = 2 * h1
    n2 = w // c
    nb = min(N2_BLOCK, n2)
    lanes = twr.shape[2]
    return pl.pallas_call(
        functools.partial(_fft1_kernel, c=c),
        grid=(n2 // nb, b // 2),
        in_specs=[pl.BlockSpec((None, h1, nb * c), lambda j, p: (2 * p, 0, j)),
                  pl.BlockSpec((None, h1, nb * c), lambda j, p: (2 * p + 1, 0, j)),
                  pl.BlockSpec((2 * n1, n1), lambda j, p: (0, 0)),
                  pl.BlockSpec((nb, n1, lanes), lambda j, p: (j, 0, 0)),
                  pl.BlockSpec((nb, n1, lanes), lambda j, p: (j, 0, 0))],
        out_specs=pl.BlockSpec((None, 2 * n1, nb * c), lambda j, p: (p, 0, j)),
        out_shape=jax.ShapeDtypeStruct((b // 2, 2 * n1, w), BF16),
        compiler_params=_cparams("arbitrary", "arbitrary"),
        name="fft1",
    )(u3, u3, f1f, twr, twi)


def _fftk_kernel(pr_ref, pi_ref, qr_ref, qi_ref, f_ref, as_ref, o_ref, *, n):
    p = _bdot(f_ref[...], jnp.concatenate([pr_ref[...], pi_ref[...]], axis=0))
    q = _bdot(f_ref[...], jnp.concatenate([qr_ref[...], qi_ref[...]], axis=0))
    n2 = pr_ref.shape[0]
    scale = 1.0 / (n * (as_ref[...] + EPS))
    o_ref[0:n2, :] = (p[0:n2] + q[0:n2]) * scale
    o_ref[n2:, :] = (p[n2:] - q[n2:]) * scale


def _fft_kernel_spectrum(s1, f2f, asum, *, n):
    _, _, n1, n2, c = s1.shape
    spec = lambda pair, ri: pl.BlockSpec((None, None, None, n2, c), lambda k: (pair, ri, k, 0, 0))
    return pl.pallas_call(
        functools.partial(_fftk_kernel, n=n),
        grid=(n1,),
        in_specs=[spec(0, 0), spec(0, 1), spec(1, 0), spec(1, 1),
                  pl.BlockSpec((2 * n2, 2 * n2), lambda k: (0, 0)),
                  pl.BlockSpec((1, c), lambda k: (0, 0))],
        out_specs=pl.BlockSpec((None, 2 * n2, c), lambda k: (k, 0, 0)),
        out_shape=jax.ShapeDtypeStruct((n1, 2 * n2, c), F32),
        compiler_params=_cparams("arbitrary"),
        name="fftk",
    )(s1, s1, s1, s1, f2f, asum)


def _fft2_kernel(ar_ref, ai_ref, kh_ref, ff_ref, fi_ref, o_ref):
    n2 = ar_ref.shape[0]
    x = _bdot(ff_ref[...], jnp.concatenate([ar_ref[...], ai_ref[...]], axis=0))
    xr, xi = x[0:n2], x[n2:]
    kr, ki = kh_ref[0:n2, :], kh_ref[n2:, :]
    y = jnp.concatenate([xr * kr - xi * ki, xr * ki + xi * kr], axis=0).astype(BF16)
    b = _bdot(fi_ref[...], y)
    o_ref[0] = b[0:n2].astype(BF16)
    o_ref[1] = b[n2:].astype(BF16)


def _fft2(s1, khat, f2f, f2i):
    pairs, _, n1, n2, c = s1.shape
    spec = lambda ri: pl.BlockSpec((None, None, None, n2, c), lambda k, p: (p, ri, k, 0, 0))
    return pl.pallas_call(
        _fft2_kernel,
        grid=(n1, pairs),
        in_specs=[spec(0), spec(1),
                  pl.BlockSpec((None, 2 * n2, c), lambda k, p: (k, 0, 0)),
                  pl.BlockSpec((2 * n2, 2 * n2), lambda k, p: (0, 0)),
                  pl.BlockSpec((2 * n2, 2 * n2), lambda k, p: (0, 0))],
        out_specs=pl.BlockSpec((None, 2, None, n2, c), lambda k, p: (p, 0, k, 0, 0)),
        out_shape=jax.ShapeDtypeStruct((pairs, 2, n1, n2, c), BF16),
        compiler_params=_cparams("arbitrary", "arbitrary"),
        name="fft2",
    )(s1, s1, khat, f2f, f2i)


def _fft3_kernel(b_ref, f_ref, twr_ref, twi_ref, o_ref, t_ref, *, c):
    n1 = twr_ref.shape[1]
    h1 = n1 // 2
    reps = c // twr_ref.shape[2]
    for j in range(twr_ref.shape[0]):
        br = b_ref[0:n1, j * c:(j + 1) * c].astype(F32)
        bi = b_ref[n1:, j * c:(j + 1) * c].astype(F32)
        wr = jnp.tile(twr_ref[j], (1, reps))
        wi = jnp.tile(twi_ref[j], (1, reps))
        t_ref[0:n1, j * c:(j + 1) * c] = (br * wr - bi * wi).astype(BF16)
        t_ref[n1:, j * c:(j + 1) * c] = (bi * wr + br * wi).astype(BF16)
    y = _bdot(f_ref[...], t_ref[...])
    o_ref[0] = y[0:h1].astype(BF16)
    o_ref[1] = y[h1:].astype(BF16)


def _fft3(s2, f1i, twr, twi, *, c):
    pairs, rows, w = s2.shape
    n1 = rows // 2
    h1 = n1 // 2
    n2 = w // c
    nb = min(N2_BLOCK, n2)
    lanes = twr.shape[2]
    return pl.pallas_call(
        functools.partial(_fft3_kernel, c=c),
        grid=(n2 // nb, pairs),
        in_specs=[pl.BlockSpec((None, 2 * n1, nb * c), lambda j, p: (p, 0, j)),
                  pl.BlockSpec((n1, 2 * n1), lambda j, p: (0, 0)),
                  pl.BlockSpec((nb, n1, lanes), lambda j, p: (j, 0, 0)),
                  pl.BlockSpec((nb, n1, lanes), lambda j, p: (j, 0, 0))],
        out_specs=pl.BlockSpec((None, 2, h1, nb * c), lambda j, p: (p, 0, 0, j)),
        out_shape=jax.ShapeDtypeStruct((pairs, 2, h1, w), BF16),
        scratch_shapes=[pltpu.VMEM((2 * n1, nb * c), BF16)],
        compiler_params=_cparams("arbitrary", "arbitrary"),
        name="fft3",
    )(s2, f1i, twr, twi)


def _hyena_long_conv(u, hf, hb, asum):
    b, l, c = u.shape
    n, n1, n2 = _fft_dims(l)
    h1 = n1 // 2
    f1f, f1i, f2f, f2i, twr, twi = _fft_tables(l, 128)
    zeros = jnp.zeros((l, c), BF16)
    taps = jnp.stack([hf.astype(BF16), zeros, hb.astype(BF16), zeros]).reshape(4, h1, n2 * c)
    s1k = _fft1(taps, f1f, twr, twi, c=c).reshape(2, 2, n1, n2, c)
    khat = _fft_kernel_spectrum(s1k, f2f, asum, n=n)
    s1 = _fft1(u.reshape(b, h1, n2 * c), f1f, twr, twi, c=c).reshape(b // 2, 2, n1, n2, c)
    s2 = _fft2(s1, khat, f2f, f2i).reshape(b // 2, 2 * n1, n2 * c)
    return _fft3(s2, f1i, twr, twi, c=c).reshape(b, l, c)


def _ret_kernel(q_ref, k_ref, v_ref, g_ref, cos_ref, sin_ref, pm_ref, lgf_ref, lgb_ref, sf0_ref, sb0_ref,
                o_ref, sf_ref, sb_ref, sbs_ref, *, nb, dk):
    s = pl.program_id(2)
    ch = RET_CHUNK
    nc = k_ref.shape[0] // ch
    lgf = lgf_ref[...]
    lgb = lgb_ref[...]
    ii = lax.broadcasted_iota(jnp.int32, (ch, dk), 0).astype(F32)
    scale = dk ** -0.5

    def rot(x_bf, r0):
        sw = _bdot(x_bf, pm_ref[...])
        return x_bf.astype(F32) * cos_ref[r0:r0 + ch, :] + sw * sin_ref[r0:r0 + ch, :]

    @pl.when(s == 0)
    def _():
        sb_ref[...] = sb0_ref[...]

    @pl.when(s == nb)
    def _():
        sf_ref[...] = sf0_ref[...]

    @pl.when(s < nb)
    def _():
        blk = nb - 1 - s
        zb = jnp.exp(lgb * ii)
        gch = jnp.exp(lgb * float(ch))
        for cc in reversed(range(nc)):
            r0 = cc * ch
            kc = rot(k_ref[r0:r0 + ch, :], r0) * scale
            sbs_ref[blk * nc + cc] = sb_ref[...]
            upd = _bdot((kc * zb).T.astype(BF16), v_ref[r0:r0 + ch, :])
            sb_ref[...] = sb_ref[...] * gch + upd

    @pl.when(s >= nb)
    def _():
        blk = s - nb
        ri = lax.broadcasted_iota(jnp.int32, (ch, ch), 0)
        ci = lax.broadcasted_iota(jnp.int32, (ch, ch), 1)
        rel = (ri - ci).astype(F32)
        dmat = jnp.where(ri > ci, jnp.exp(lgf * jnp.maximum(rel, 0.0)),
                         jnp.where(ri < ci, jnp.exp(lgb * jnp.maximum(-rel, 0.0)), 2.0))
        xf = jnp.exp(lgf * (ii + 1.0))
        xb = jnp.exp(lgb * (float(ch) - ii))
        zf = jnp.exp(lgf * (float(ch) - 1.0 - ii))
        gch = jnp.exp(lgf * float(ch))
        for cc in range(nc):
            r0 = cc * ch
            qc = rot(q_ref[r0:r0 + ch, :], r0)
            kc = rot(k_ref[r0:r0 + ch, :], r0) * scale
            vc = v_ref[r0:r0 + ch, :]
            sc = _bdot_nt(qc.astype(BF16), kc.astype(BF16)) * dmat
            qx = jnp.concatenate([qc * xf, qc * xb], axis=1).astype(BF16)
            st = jnp.concatenate([sf_ref[...], sbs_ref[blk * nc + cc]], axis=0).astype(BF16)
            o = _bdot(sc.astype(BF16), vc) + _bdot(qx, st)
            sf_ref[...] = sf_ref[...] * gch + _bdot((kc * zf).T.astype(BF16), vc)
            y = o * lax.rsqrt(jnp.mean(o * o, axis=-1, keepdims=True) + EPS)
            o_ref[r0:r0 + ch, :] = (y * _silu(g_ref[r0:r0 + ch, :].astype(F32))).astype(BF16)


def _retention(p, cosf, sinf, pm, lgf, lgb, sf0, sb0, *, q_off, k_off, v_off, g_off, heads, dk, dv, tb):
    b, l, _ = p.shape
    nb = l // tb
    fwd = lambda s: jnp.maximum(s - nb, 0)
    both = lambda s: jnp.where(s < nb, nb - 1 - s, s - nb)
    return pl.pallas_call(
        functools.partial(_ret_kernel, nb=nb, dk=dk),
        grid=(b, heads, 2 * nb),
        in_specs=[pl.BlockSpec((None, tb, dk), lambda i, h, s: (i, fwd(s), q_off // dk + h)),
                  pl.BlockSpec((None, tb, dk), lambda i, h, s: (i, both(s), k_off // dk + h)),
                  pl.BlockSpec((None, tb, dv), lambda i, h, s: (i, both(s), v_off // dv + h)),
                  pl.BlockSpec((None, tb, dv), lambda i, h, s: (i, fwd(s), g_off // dv + h)),
                  pl.BlockSpec((tb, dk), lambda i, h, s: (both(s), 0)),
                  pl.BlockSpec((tb, dk), lambda i, h, s: (both(s), 0)),
                  pl.BlockSpec((dk, dk), lambda i, h, s: (0, 0)),
                  pl.BlockSpec((None, 1, 1), lambda i, h, s: (h, 0, 0)),
                  pl.BlockSpec((None, 1, 1), lambda i, h, s: (h, 0, 0)),
                  pl.BlockSpec((None, None, dk, dv), lambda i, h, s: (i, h, 0, 0)),
                  pl.BlockSpec((None, None, dk, dv), lambda i, h, s: (i, h, 0, 0))],
        out_specs=pl.BlockSpec((None, tb, dv), lambda i, h, s: (i, fwd(s), h)),
        out_shape=jax.ShapeDtypeStruct((b, l, heads * dv), BF16),
        scratch_shapes=[pltpu.VMEM((dk, dv), F32), pltpu.VMEM((dk, dv), F32),
                        pltpu.VMEM((l // RET_CHUNK, dk, dv), F32)],
        compiler_params=_cparams("arbitrary", "arbitrary", "arbitrary"),
        name="ret",
    )(p, p, p, p, cosf, sinf, pm, lgf, lgb, sf0, sb0)


def _merge_kernel(y_ref, u_ref, x0_ref, r_ref, ah_ref, ar_ref, x_ref, g1_ref, sh_ref, sc_ref, n2_ref, hb_ref,
                  wh_ref, wr_ref, wo_ref, x2_ref, h2_ref):
    yh = (y_ref[...].astype(F32) + u_ref[...].astype(F32) * hb_ref[...]) * x0_ref[...].astype(F32)
    y_h = _bdot(yh.astype(BF16), wh_ref[...])
    y_r = _bdot(r_ref[...], wr_ref[...])
    m = _sigmoid(ah_ref[...].astype(F32)) * y_h + _sigmoid(ar_ref[...].astype(F32)) * y_r
    x2 = x_ref[...] + g1_ref[...] * _bdot(m.astype(BF16), wo_ref[...])
    x2_ref[...] = x2
    h2_ref[...] = (_rms(x2, n2_ref[...]) * (1.0 + sc_ref[...]) + sh_ref[...]).astype(BF16)


def _merge(y, u, x0, r, p, x, g1, sh2, sc2, norm2, hy_bias, wh, wr, wo, *, ah_off, ar_off, tm):
    b, l, d = x.shape
    c = y.shape[2]
    rw = r.shape[2]
    tile = lambda w, cb=0: pl.BlockSpec((None, tm, w), lambda i, j: (i, j, cb))
    vec = pl.BlockSpec((None, 1, d), lambda i, j: (i, 0, 0))
    return pl.pallas_call(
        _merge_kernel,
        grid=(b, l // tm),
        in_specs=[tile(c), tile(c), tile(c), tile(rw), tile(d, ah_off // d), tile(d, ar_off // d), tile(d),
                  vec, vec, vec,
                  pl.BlockSpec((1, d), lambda i, j: (0, 0)),
                  pl.BlockSpec((1, c), lambda i, j: (0, 0)),
                  _const_spec((c, d)), _const_spec((rw, d)), _const_spec((d, d))],
        out_specs=[tile(d), tile(d)],
        out_shape=[jax.ShapeDtypeStruct((b, l, d), F32), jax.ShapeDtypeStruct((b, l, d), BF16)],
        compiler_params=_cparams("arbitrary", "arbitrary"),
        name="merge",
    )(y, u, x0, r, p, p, x, g1, sh2, sc2, norm2, hy_bias, wh, wr, wo)


def _top_sorted(s, k):
    vals = []
    rank = jnp.full(s.shape, float(k), F32)
    for r in range(k):
        m = jnp.max(s, axis=0, keepdims=True)
        vals.append(m)
        hit = s == m
        rank = jnp.where(hit, float(r), rank)
        s = jnp.where(hit, NEG_INF, s)
    return vals, rank


def _pair_pack(v):
    b = pltpu.bitcast(v.astype(BF16).astype(F32), jnp.uint32)
    return b | (b >> 16)


def _row_bcast(row, rows):
    rep = pltpu.bitcast(jnp.broadcast_to(row, (8, row.shape[1])), BF16)
    return jnp.tile(rep, (rows // 16, 1))


def _peer_pairs(k):
    return [(i, j) for i in range(k) for j in range(k) if (i + 1) * (j + 1) <= k]


def _peer_kernel(h_ref, x_ref, g2_ref, fn_ref, wq_ref, kd_ref, u_ref, vt_ref, o_ref,
                 cn_ref, e1_ref, rk_ref, e2_ref, q_ref, cand_ref, a_ref, w_ref, acc_ref, *, topk, split):
    j = pl.program_id(2)
    nk = PEER_N_KEYS
    t = h_ref.shape[0]
    eb = u_ref.shape[0]
    pairs = _peer_pairs(topk)

    @pl.when(j == 0)
    def _():
        q = _bdot(h_ref[...], wq_ref[...]).astype(BF16)
        for hd in range(PEER_HEADS):
            q_ref[hd] = q[:, hd * PEER_DK:(hd + 1) * PEER_DK]
        cand_ref[...] = jnp.full(cand_ref.shape, NEG_INF, F32)

        def head(hd, carry):
            st = _bdot_nt(kd_ref[hd], q_ref[hd])
            s1, s2 = st[0:nk], st[nk:]
            a, rank1 = _top_sorted(s1, topk)
            b, rank2 = _top_sorted(s2, topk)
            for r, (pi, pj) in enumerate(pairs):
                cand_ref[r:r + 1, :] = a[pi] + b[pj]
            cs = cand_ref[...]
            tau = _top_sorted(cs, topk)[0][-1]
            sel = cs >= tau
            mx = a[0] + b[0]
            z = jnp.sum(jnp.where(sel, jnp.exp(cs - mx), 0.0), axis=0, keepdims=True)
            self = jnp.where(sel, 1.0, 0.0)
            cnt1 = jnp.zeros(s1.shape, F32)
            for i in range(topk):
                rws = [r for r, (pi, _) in enumerate(pairs) if pi == i]
                cnt_i = jnp.sum(self[rws[0]:rws[-1] + 1], axis=0, keepdims=True)
                cnt1 = jnp.where(rank1 == float(i), cnt_i, cnt1)
            cn_ref[hd] = _pair_pack(cnt1)
            e1_ref[hd] = _pair_pack(jnp.exp(s1 - a[0]))
            rk_ref[hd] = rank2.astype(BF16)
            e2_ref[hd] = (jnp.exp(s2 - b[0]) / z).astype(BF16)
            return carry

        lax.fori_loop(0, PEER_HEADS, head, 0)
        acc_ref[...] = jnp.zeros_like(acc_ref)

    part = eb // split
    zero = jnp.zeros((), BF16)
    for s in range(split):
        rows = slice(s * part, (s + 1) * part)
        a_ref[rows, :] = _bdot_nt(u_ref[rows, :], h_ref[...])
    for s in range(split):
        rows = slice(s * part, (s + 1) * part)
        for g in range(part // nk):
            r0 = s * part + g * nk
            i = j * (eb // nk) + r0 // nk
            act = _gelu_tanh(a_ref[r0:r0 + nk, :].astype(BF16))
            gate = jnp.zeros((nk, t), BF16)
            for hd in range(PEER_HEADS):
                cnt = _row_bcast(cn_ref[hd, pl.ds(i, 1), :], nk)
                e1 = _row_bcast(e1_ref[hd, pl.ds(i, 1), :], nk)
                gate = gate + jnp.where(rk_ref[hd] < cnt, e2_ref[hd], zero) * e1
            w_ref[r0:r0 + nk, :] = gate * act
        acc_ref[...] += _bdot(vt_ref[:, rows], w_ref[rows, :])

    @pl.when(j == pl.num_programs(2) - 1)
    def _():
        x3 = x_ref[...] + g2_ref[...] * acc_ref[...].T
        o_ref[...] = _rms(x3, fn_ref[...])


def _peer(h2, x2, g2, fnorm, wq, kd, u_bf, vt_bf, *, t, eb):
    b, l, d = x2.shape
    e = u_bf.shape[0]
    nk = PEER_N_KEYS
    npair = -(-len(_peer_pairs(PEER_TOPK)) // 8) * 8
    nj = e // eb
    tile =pl.BlockSpec((None, t, d), lambda i, m, j: (i, m, 0))
    return pl.pallas_call(
        functools.partial(_peer_kernel, topk=PEER_TOPK, split=4),
        grid=(b, l // t, nj),
        in_specs=[tile, tile,
                  pl.BlockSpec((None, 1, d), lambda i, m, j: (i, 0, 0)),
                  pl.BlockSpec((1, d), lambda i, m, j: (0, 0)),
                  _const_spec((d, PEER_HEADS * PEER_DK)),
                  _const_spec((PEER_HEADS, 2 * nk, PEER_DK)),
                  pl.BlockSpec((eb, d), lambda i, m, j: (j, 0)),
                  pl.BlockSpec((d, eb), lambda i, m, j: (0, j))],
        out_specs=tile,
        out_shape=jax.ShapeDtypeStruct((b, l, d), F32),
        scratch_shapes=[pltpu.VMEM((PEER_HEADS, nk, t), jnp.uint32), pltpu.VMEM((PEER_HEADS, nk, t), jnp.uint32),
                        pltpu.VMEM((PEER_HEADS, nk, t), BF16), pltpu.VMEM((PEER_HEADS, nk, t), BF16),
                        pltpu.VMEM((PEER_HEADS, t, PEER_DK), BF16),
                          pltpu.VMEM((npair, t), F32),
                          pltpu.VMEM((eb, t), F32), pltpu.VMEM((eb, t), BF16),
                          pltpu.VMEM((d, t), F32)],
        compiler_params=_cparams("arbitrary", "arbitrary", "arbitrary"),
        name="peer",
    )(h2, x2, g2, fnorm, wq, kd, u_bf, vt_bf)


def _filter_features(l, emb):
    bands_n = (emb - 1) // 2
    t = jnp.linspace(0.0, 1.0, l, dtype=F32)[:, None]
    bands = jnp.linspace(1e-4, bands_n - 1, bands_n, dtype=F32)[None, :]
    w = (2.0 * math.pi / l) * jnp.arange(l, dtype=F32)[:, None]
    z = jnp.concatenate([t, jnp.cos(bands * w), -jnp.sin(bands * w)], axis=-1)
    return t, z


def _rotary_tables(l, dk):
    half = dk // 2
    nf = half // 2
    inv = ROPE_BASE ** (-jnp.arange(nf, dtype=F32) / nf)
    pos = jnp.arange(l, dtype=jnp.int32)
    rows = (pos // GRID_W).astype(F32)
    cols = (pos % GRID_W).astype(F32)
    ar = rows[:, None] * inv[None, :]
    ac = cols[:, None] * inv[None, :]
    cosf = jnp.concatenate([jnp.cos(ar), jnp.cos(ar), jnp.cos(ac), jnp.cos(ac)], axis=1)
    sinf = jnp.concatenate([-jnp.sin(ar), jnp.sin(ar), -jnp.sin(ac), jnp.sin(ac)], axis=1)
    lane = jnp.arange(dk, dtype=jnp.int32)
    pm = (lane[:, None] == (lane[None, :] ^ nf)).astype(BF16)
    return cosf, sinf, pm


def kernel(x, c, ctx, c_ctx, w_ada, b_ada, norm1, norm2, w_in, hy_conv_w, hy_conv_b, hy_fw1, hy_fb1, hy_fw2, hy_fb2, hy_fw3, hy_fb3, hy_fw4, hy_sin_freq, hy_deltas, hy_bias, ret_log_decay_f, ret_log_decay_b, w_hy_out, w_ret_out, w_o, peer_w_query, peer_sub_keys, peer_u, peer_v, final_norm):
    assert w_ada.shape[0] == 1, "single-layer configuration"
    b, l, d = x.shape
    assert b % 2 == 0 and l % RET_CHUNK == 0
    heads = RET_HEADS
    dk = d // 8
    dv = 2 * dk
    c_hy = d
    hy_cols = 3 * c_hy
    q_off = hy_cols
    k_off = q_off + heads * dk
    v_off = k_off + heads * dk
    g_off = v_off + heads * dv
    ah_off = g_off + heads * dv
    ar_off = ah_off + d

    rows = -(-(b + 1) // 8) * 8
    cc = jnp.zeros((rows, d), F32).at[:b].set(c).at[b].set(c_ctx)
    mod = _mod(cc, w_ada[0], b_ada[0][None, :])
    mod_l = mod[:b].reshape(b, 6, 1, d)
    sh1, sc1, g1, sh2, sc2, g2 = (mod_l[:, i] for i in range(6))
    mod_c = mod[b].reshape(6, 1, d)
    csh1, csc1 = mod_c[0], mod_c[1]

    w_in_bf = w_in[0].astype(BF16)
    n1g = norm1[0][None, :]
    lgf = ret_log_decay_f[0]
    lgb = ret_log_decay_b[0]

    sf0, sb0 = _ctx_states(ctx, csh1, csc1, n1g, w_in_bf, lgf[:, None], lgb[:, None],
                           k_off=k_off, v_off=v_off, heads=heads, dk=dk, dv=dv)

    p = _inproj(x, sh1, sc1, n1g, w_in_bf, tm=min(512, l))

    u, x0c = _hy_pre(p, hy_conv_w[0], hy_conv_b[0][None, :], c=c_hy, tl=min(256, l))
    emb = hy_fw1.shape[1]
    order = hy_fw1.shape[2]
    t_lin, z = _filter_features(l, emb)
    epad = -(-emb // 64) * 64
    z = jnp.pad(z, ((0, 0), (0, epad - emb)))
    fw1 = jnp.pad(hy_fw1[0], ((0, epad - emb), (0, 0)))
    hf, hb, asum = _hy_filter(z, t_lin, fw1, hy_fb1[0][None, :], hy_fw2[0], hy_fb2[0][None, :],
                              hy_fw3[0], hy_fb3[0][None, :], hy_fw4[0], hy_sin_freq[0][None, :],
                              hy_deltas[0][None, :], tl=min(1024, l))
    del order
    y = _hyena_long_conv(u, hf, hb, asum)

    cosf, sinf, pm = _rotary_tables(l, dk)
    r = _retention(p, cosf, sinf, pm, lgf[:, None, None], lgb[:, None, None], sf0, sb0,
                   q_off=q_off, k_off=k_off, v_off=v_off, g_off=g_off, heads=heads, dk=dk, dv=dv,
                   tb=min(1024, l))

    x2, h2 = _merge(y, u, x0c, r, p, x, g1, sh2, sc2, norm2[0][None, :], hy_bias[0][None, :],
                    w_hy_out[0].astype(BF16), w_ret_out[0].astype(BF16), w_o[0].astype(BF16),
                    ah_off=ah_off, ar_off=ar_off, tm=min(512, l))

    sk = peer_sub_keys[0]
    hk = sk.shape[3]
    zk = jnp.zeros_like(sk[:, 0])
    kd = jnp.concatenate([jnp.concatenate([sk[:, 0], zk], axis=2),
                          jnp.concatenate([zk, sk[:, 1]], axis=2)], axis=1).astype(BF16)
    del hk
    out = _peer(h2, x2, g2, final_norm[None, :], peer_w_query[0].astype(BF16), kd,
                peer_u[0].astype(BF16), peer_v[0].T.astype(BF16), t=min(512, l), eb=2048)
    return out
```

```python
import functools
import math

import jax
import jax.numpy as jnp
import numpy as np
from jax import lax
from jax.experimental import pallas as pl
from jax.experimental.pallas import tpu as pltpu

F32 = jnp.float32
BF16 = jnp.bfloat16
EPS = 1e-6

GRID_W = 64
RET_HEADS = 4
RET_CHUNK = 128
ROPE_BASE = 10000.0
PEER_HEADS = 8
PEER_N_KEYS = 128
PEER_TOPK = 16
PEER_DK = 128

VMEM_LIMIT_BYTES = 56 * 1024 * 1024
LANES = 128
MXU = 256
NEG_INF = float("-inf")


def _cparams(*sem):
    return pltpu.CompilerParams(dimension_semantics=sem, vmem_limit_bytes=VMEM_LIMIT_BYTES)


def _bdot(a, b):
    return jnp.dot(a, b, preferred_element_type=F32)


def _bdot_nt(a, b):
    return lax.dot_general(a, b, (((1,), (1,)), ((), ())), preferred_element_type=F32)


def _split(a):
    hi = a.astype(BF16)
    lo = (a - hi.astype(F32)).astype(BF16)
    return hi, lo


def _dot3(a, b):
    ah, al = _split(a)
    bh, bl = _split(b)
    return _bdot(ah, bh) + _bdot(ah, bl) + _bdot(al, bh)


def _sigmoid(x):
    return 1.0 / (1.0 + jnp.exp(-x))


def _silu(x):
    return x * _sigmoid(x)


def _gelu_tanh(x):
    k = 2.0 * math.sqrt(2.0 / math.pi) * math.log2(math.e)
    return x / (1.0 + jnp.exp2(-(x * (k + (k * 0.044715) * (x * x)))))


def _rms(x, gain):
    return x * lax.rsqrt(jnp.mean(x * x, axis=-1, keepdims=True) + EPS) * gain


def _const_spec(shape):
    nd = len(shape)
    return pl.BlockSpec(shape, lambda *_: (0,) * nd, pipeline_mode=pl.Buffered(1))


def _mod_kernel(c_ref, w_ref, b_ref, o_ref):
    o_ref[...] = _dot3(_silu(c_ref[...]), w_ref[...]) + b_ref[...]


def _mod(cc, w_ada, b_ada):
    rows, d = cc.shape
    n = w_ada.shape[1]
    return pl.pallas_call(
        _mod_kernel,
        grid=(n // d,),
        in_specs=[pl.BlockSpec((rows, d), lambda j: (0, 0)),
                  pl.BlockSpec((d, d), lambda j: (0, j)),
                  pl.BlockSpec((1, d), lambda j: (0, j))],
        out_specs=pl.BlockSpec((rows, d), lambda j: (0, j)),
        out_shape=jax.ShapeDtypeStruct((rows, n), F32),
        compiler_params=_cparams("arbitrary"),
        name="mod",
    )(cc, w_ada, b_ada)


def _ctx_kernel(ctx_ref, sh_ref, sc_ref, g_ref, wk_ref, wv_ref, lgf_ref, lgb_ref, sf_ref, sb_ref, *, heads, dk, dv):
    n = ctx_ref.shape[0]
    h = (_rms(ctx_ref[...], g_ref[...]) * (1.0 + sc_ref[...]) + sh_ref[...]).astype(BF16)
    kc = _bdot(h, wk_ref[...]) * (dk ** -0.5)
    vc = _bdot(h, wv_ref[...]).astype(BF16)
    pos = lax.broadcasted_iota(jnp.int32, (n, dk), 0).astype(F32)
    for hd in range(heads):
        kh = kc[:, hd * dk:(hd + 1) * dk]
        vh = vc[:, hd * dv:(hd + 1) * dv]
        wf = jnp.exp(lgf_ref[hd:hd + 1, :] * (n - 1.0 - pos))
        wb = jnp.exp(lgb_ref[hd:hd + 1, :] * pos)
        sf_ref[hd] = _bdot((kh * wf).T.astype(BF16), vh)
        sb_ref[hd] = _bdot((kh * wb).T.astype(BF16), vh)


def _ctx_states(ctx, csh, csc, gain, w_in_bf, lgf, lgb, *, k_off, v_off, heads, dk, dv):
    b, n, d = ctx.shape
    kw, vw = heads * dk, heads * dv
    out = jax.ShapeDtypeStruct((b, heads, dk, dv), F32)
    return pl.pallas_call(
        functools.partial(_ctx_kernel, heads=heads, dk=dk, dv=dv),
        grid=(b,),
        in_specs=[pl.BlockSpec((None, n, d), lambda i: (i, 0, 0)),
                  pl.BlockSpec((1, d), lambda i: (0, 0)),
                  pl.BlockSpec((1, d), lambda i: (0, 0)),
                  pl.BlockSpec((1, d), lambda i: (0, 0)),
                  pl.BlockSpec((d, kw), lambda i: (0, k_off // kw)),
                  pl.BlockSpec((d, vw), lambda i: (0, v_off // vw)),
                  pl.BlockSpec((heads, 1), lambda i: (0, 0)),
                  pl.BlockSpec((heads, 1), lambda i: (0, 0))],
        out_specs=[pl.BlockSpec((None, heads, dk, dv), lambda i: (i, 0, 0, 0)),
                   pl.BlockSpec((None, heads, dk, dv), lambda i: (i, 0, 0, 0))],
        out_shape=[out, out],
        compiler_params=_cparams("arbitrary"),
        name="ctx",
    )(ctx, csh, csc, gain, w_in_bf, w_in_bf, lgf, lgb)


def _inproj_kernel(x_ref, sh_ref, sc_ref, g_ref, w_ref, o_ref, *, nc):
    h = (_rms(x_ref[...], g_ref[...]) * (1.0 + sc_ref[...]) + sh_ref[...]).astype(BF16)
    n = w_ref.shape[1]
    for j in range(n // nc):
        o_ref[:, j * nc:(j + 1) * nc] = _bdot(h, w_ref[:, j * nc:(j + 1) * nc]).astype(BF16)


def _inproj(x, sh, sc, gain, w_bf, *, tm):
    b, l, d = x.shape
    n = w_bf.shape[1]
    return pl.pallas_call(
        functools.partial(_inproj_kernel, nc=1024),
        grid=(b, l // tm),
        in_specs=[pl.BlockSpec((None, tm, d), lambda i, j: (i, j, 0)),
                  pl.BlockSpec((None, 1, d), lambda i, j: (i, 0, 0)),
                  pl.BlockSpec((None, 1, d), lambda i, j: (i, 0, 0)),
                  pl.BlockSpec((1, d), lambda i, j: (0, 0)),
                  _const_spec((d, n))],
        out_specs=pl.BlockSpec((None, tm, n), lambda i, j: (i, j, 0)),
        out_shape=jax.ShapeDtypeStruct((b, l, n), BF16),
        compiler_params=_cparams("arbitrary", "arbitrary"),
        name="inproj",
    )(x, sh, sc, gain, w_bf)


HALO = 16


def _hy_pre_kernel(p_ref, pp_ref, pn_ref, w_ref, b_ref, u_ref, x0_ref, *, c, cw):
    i = pl.program_id(1)
    first = i == 0
    last = i == pl.num_programs(1) - 1
    tl = p_ref.shape[0]
    row = lax.broadcasted_iota(jnp.int32, (tl, cw), 0)
    for cb in range(c // cw):
        zs = []
        for part in range(3):
            c0 = part * c + cb * cw
            cur = p_ref[:, c0:c0 + cw].astype(F32)
            prev = pp_ref[:, c0:c0 + cw].astype(F32)[HALO - 1:HALO, :]
            nxt = pn_ref[:, c0:c0 + cw].astype(F32)[0:1, :]
            prev = jnp.where(first, 0.0, prev)
            nxt = jnp.where(last, 0.0, nxt)
            up = jnp.where(row == 0, prev, pltpu.roll(cur, 1, 0))
            dn = jnp.where(row == tl - 1, nxt, pltpu.roll(cur, tl - 1, 0))
            w = w_ref[:, c0:c0 + cw]
            zs.append(up * w[0:1, :] + cur * w[1:2, :] + dn * w[2:3, :] + b_ref[:, c0:c0 + cw])
        x0, x1, v = zs
        u_ref[:, cb * cw:(cb + 1) * cw] = (v * x1).astype(BF16)
        x0_ref[:, cb * cw:(cb + 1) * cw] = x0.astype(BF16)


def _hy_pre(p, conv_w, conv_b, *, c, tl):
    b, l, _ = p.shape
    hb = tl // HALO
    nh = l // HALO
    out = jax.ShapeDtypeStruct((b, l, c), BF16)
    return pl.pallas_call(
        functools.partial(_hy_pre_kernel, c=c, cw=256),
        grid=(b, l // tl),
        in_specs=[pl.BlockSpec((None, tl, 3 * c), lambda i, j: (i, j, 0)),
                  pl.BlockSpec((None, HALO, 3 * c), lambda i, j: (i, jnp.maximum(j * hb - 1, 0), 0)),
                  pl.BlockSpec((None, HALO, 3 * c), lambda i, j: (i, jnp.minimum((j + 1) * hb, nh - 1), 0)),
                  pl.BlockSpec((3, 3 * c), lambda i, j: (0, 0)),
                  pl.BlockSpec((1, 3 * c), lambda i, j: (0, 0))],
        out_specs=[pl.BlockSpec((None, tl, c), lambda i, j: (i, j, 0)),
                   pl.BlockSpec((None, tl, c), lambda i, j: (i, j, 0))],
        out_shape=[out, out],
        compiler_params=_cparams("arbitrary", "arbitrary"),
        name="hy_pre",
    )(p, p, p, conv_w, conv_b)


def _filt_kernel(z_ref, t_ref, w1, b1, w2, b2, w3, b3, w4, fr, dl, hf_ref, hb_ref, as_ref, *, c):
    i = pl.program_id(0)
    tl = z_ref.shape[0]
    f = fr[...]
    h = jnp.sin(f * (_dot3(z_ref[...], w1[...]) + b1[...]))
    h = jnp.sin(f * (_dot3(h, w2[...]) + b2[...]))
    h = jnp.sin(f * (_dot3(h, w3[...]) + b3[...]))
    h4 = _dot3(h, w4[...])
    win = jnp.exp(-t_ref[...] * jnp.abs(dl[...]))
    hf = h4[:, :c] * win
    row = lax.broadcasted_iota(jnp.int32, (tl, c), 0) + i * tl
    hb = jnp.where(row == 0, 0.0, h4[:, c:] * win)

    hf_ref[...] = hf
    hb_ref[...] = hb

    @pl.when(i == 0)
    def _():
        as_ref[...] = jnp.zeros_like(as_ref)

    as_ref[...] += jnp.sum(jnp.abs(hf) + jnp.abs(hb), axis=0, keepdims=True)


def _hy_filter(z, t, w1, b1, w2, b2, w3, b3, w4, fr, dl, *, tl):
    l, e = z.shape
    o = w2.shape[0]
    c = dl.shape[1]
    full = lambda shape: pl.BlockSpec(shape, lambda i: (0, 0))
    return pl.pallas_call(
        functools.partial(_filt_kernel, c=c),
        grid=(l // tl,),
        in_specs=[pl.BlockSpec((tl, e), lambda i: (i, 0)),
                  pl.BlockSpec((tl, 1), lambda i: (i, 0)),
                  full((e, o)), full((1, o)), full((o, o)), full((1, o)), full((o, o)), full((1, o)),
                  full((o, 2 * c)), full((1, o)), full((1, c))],
        out_specs=[pl.BlockSpec((tl, c), lambda i: (i, 0)),
                   pl.BlockSpec((tl, c), lambda i: (i, 0)),
                   pl.BlockSpec((1, c), lambda i: (0, 0))],
        out_shape=[jax.ShapeDtypeStruct((l, c), F32), jax.ShapeDtypeStruct((l, c), F32),
                   jax.ShapeDtypeStruct((1, c), F32)],
        compiler_params=_cparams("arbitrary"),
        name="filt",
    )(z, t, w1, b1, w2, b2, w3, b3, w4, fr, dl)


N2_BLOCK = 8


def _fft_dims(l):
    n = 2 * l
    n1 = math.isqrt(n)
    assert n1 * n1 == n and n1 % 32 == 0, "sequence length must make 2L a square of a multiple of 32"
    return n, n1, n1


def _fft_tables(l, lanes):
    n, n1, n2 = _fft_dims(l)
    h1 = n1 // 2
    k1 = jnp.arange(n1, dtype=jnp.int32)
    th1 = ((k1[:, None] * jnp.arange(h1, dtype=jnp.int32)[None, :]) % n1).astype(F32) * (2.0 * math.pi / n1)
    c1, s1 = jnp.cos(th1), jnp.sin(th1)
    f1f = jnp.concatenate([jnp.concatenate([c1, s1], axis=1), jnp.concatenate([-s1, c1], axis=1)], axis=0)
    f1i = f1f.T
    k2 = jnp.arange(n2, dtype=jnp.int32)
    th2 = ((k2[:, None] * k2[None, :]) % n2).astype(F32) * (2.0 * math.pi / n2)
    c2, s2 = jnp.cos(th2), jnp.sin(th2)
    f2f = jnp.concatenate([jnp.concatenate([c2, s2], axis=1), jnp.concatenate([-s2, c2], axis=1)], axis=0)
    f2i = jnp.concatenate([jnp.concatenate([c2, -s2], axis=1), jnp.concatenate([s2, c2], axis=1)], axis=0)
    tht = (k2[:, None] * k1[None, :]).astype(F32) * (2.0 * math.pi / n)
    twr = jnp.broadcast_to(jnp.cos(tht)[:, :, None], (n2, n1, lanes))
    twi = jnp.broadcast_to(jnp.sin(tht)[:, :, None], (n2, n1, lanes))
    return f1f.astype(BF16), f1i.astype(BF16), f2f.astype(BF16), f2i.astype(BF16), twr, twi


def _fft1_kernel(xr_ref, xi_ref, f_ref, twr_ref, twi_ref, o_ref, *, c):
    n1 = twr_ref.shape[1]
    reps = c // twr_ref.shape[2]
    x = jnp.concatenate([xr_ref[...], xi_ref[...]], axis=0)
    a = _bdot(f_ref[...], x)
    for j in range(twr_ref.shape[0]):
        ar = a[0:n1, j * c:(j + 1) * c]
        ai = a[n1:, j * c:(j + 1) * c]
        wr = jnp.tile(twr_ref[j], (1, reps))
        wi = jnp.tile(twi_ref[j], (1, reps))
        o_ref[0:n1, j * c:(j + 1) * c] = (ar * wr + ai * wi).astype(BF16)
        o_ref[n1:, j * c:(j + 1) * c] = (ai * wr - ar * wi).astype(BF16)


def _fft1(u3, f1f, twr, twi, *, c):
    b, h1, w = u3.shape
    n1 = 2 * h1
    n2 = w // c
    nb = min(N2_BLOCK, n2)
    lanes = twr.shape[2]
    return pl.pallas_call(
        functools.partial(_fft1_kernel, c=c),
        grid=(n2 // nb, b // 2),
        in_specs=[pl.BlockSpec((None, h1, nb * c), lambda j, p: (2 * p, 0, j)),
                  pl.BlockSpec((None, h1, nb * c), lambda j, p: (2 * p + 1, 0, j)),
                  pl.BlockSpec((2 * n1, n1), lambda j, p: (0, 0)),
                  pl.BlockSpec((nb, n1, lanes), lambda j, p: (j, 0, 0)),
                  pl.BlockSpec((nb, n1, lanes), lambda j, p: (j, 0, 0))],
        out_specs=pl.BlockSpec((None, 2 * n1, nb * c), lambda j, p: (p, 0, j)),
        out_shape=jax.ShapeDtypeStruct((b // 2, 2 * n1, w), BF16),
        compiler_params=_cparams("arbitrary", "arbitrary"),
        name="fft1",
    )(u3, u3, f1f, twr, twi)


def _fftk_kernel(pr_ref, pi_ref, qr_ref, qi_ref, f_ref, as_ref, o_ref, *, n):
    p = _bdot(f_ref[...], jnp.concatenate([pr_ref[...], pi_ref[...]], axis=0))
    q = _bdot(f_ref[...], jnp.concatenate([qr_ref[...], qi_ref[...]], axis=0))
    n2 = pr_ref.shape[0]
    scale = 1.0 / (n * (as_ref[...] + EPS))
    o_ref[0:n2, :] = (p[0:n2] + q[0:n2]) * scale
    o_ref[n2:, :] = (p[n2:] - q[n2:]) * scale


def _fft_kernel_spectrum(s1, f2f, asum, *, n):
    _, _, n1, n2, c = s1.shape
    spec = lambda pair, ri: pl.BlockSpec((None, None, None, n2, c), lambda k: (pair, ri, k, 0, 0))
    return pl.pallas_call(
        functools.partial(_fftk_kernel, n=n),
        grid=(n1,),
        in_specs=[spec(0, 0), spec(0, 1), spec(1, 0), spec(1, 1),
                  pl.BlockSpec((2 * n2, 2 * n2), lambda k: (0, 0)),
                  pl.BlockSpec((1, c), lambda k: (0, 0))],
        out_specs=pl.BlockSpec((None, 2 * n2, c), lambda k: (k, 0, 0)),
        out_shape=jax.ShapeDtypeStruct((n1, 2 * n2, c), F32),
        compiler_params=_cparams("arbitrary"),
        name="fftk",
    )(s1, s1, s1, s1, f2f, asum)


def _fft2_kernel(ar_ref, ai_ref, kh_ref, ff_ref, fi_ref, o_ref):
    n2 = ar_ref.shape[0]
    x = _bdot(ff_ref[...], jnp.concatenate([ar_ref[...], ai_ref[...]], axis=0))
    xr, xi = x[0:n2], x[n2:]
    kr, ki = kh_ref[0:n2, :], kh_ref[n2:, :]
    y = jnp.concatenate([xr * kr - xi * ki, xr * ki + xi * kr], axis=0).astype(BF16)
    b = _bdot(fi_ref[...], y)
    o_ref[0] = b[0:n2].astype(BF16)
    o_ref[1] = b[n2:].astype(BF16)


def _fft2(s1, khat, f2f, f2i):
    pairs, _, n1, n2, c = s1.shape
    spec = lambda ri: pl.BlockSpec((None, None, None, n2, c), lambda k, p: (p, ri, k, 0, 0))
    return pl.pallas_call(
        _fft2_kernel,
        grid=(n1, pairs),
        in_specs=[spec(0), spec(1),
                  pl.BlockSpec((None, 2 * n2, c), lambda k, p: (k, 0, 0)),
                  pl.BlockSpec((2 * n2, 2 * n2), lambda k, p: (0, 0)),
                  pl.BlockSpec((2 * n2, 2 * n2), lambda k, p: (0, 0))],
        out_specs=pl.BlockSpec((None, 2, None, n2, c), lambda k, p: (p, 0, k, 0, 0)),
        out_shape=jax.ShapeDtypeStruct((pairs, 2, n1, n2, c), BF16),
        compiler_params=_cparams("arbitrary", "arbitrary"),
        name="fft2",
    )(s1, s1, khat, f2f, f2i)


def _fft3_kernel(b_ref, f_ref, twr_ref, twi_ref, o_ref, t_ref, *, c):
    n1 = twr_ref.shape[1]
    h1 = n1 // 2
    reps = c // twr_ref.shape[2]
    for j in range(twr_ref.shape[0]):
        br = b_ref[0:n1, j * c:(j + 1) * c].astype(F32)
        bi = b_ref[n1:, j * c:(j + 1) * c].astype(F32)
        wr = jnp.tile(twr_ref[j], (1, reps))
        wi = jnp.tile(twi_ref[j], (1, reps))
        t_ref[0:n1, j * c:(j + 1) * c] = (br * wr - bi * wi).astype(BF16)
        t_ref[n1:, j * c:(j + 1) * c] = (bi * wr + br * wi).astype(BF16)
    y = _bdot(f_ref[...], t_ref[...])
    o_ref[0] = y[0:h1].astype(BF16)
    o_ref[1] = y[h1:].astype(BF16)


def _fft3(s2, f1i, twr, twi, *, c):
    pairs, rows, w = s2.shape
    n1 = rows // 2
    h1 = n1 // 2
    n2 = w // c
    nb = min(N2_BLOCK, n2)
    lanes = twr.shape[2]
    return pl.pallas_call(
        functools.partial(_fft3_kernel, c=c),
        grid=(n2 // nb, pairs),
        in_specs=[pl.BlockSpec((None, 2 * n1, nb * c), lambda j, p: (p, 0, j)),
                  pl.BlockSpec((n1, 2 * n1), lambda j, p: (0, 0)),
                  pl.BlockSpec((nb, n1, lanes), lambda j, p: (j, 0, 0)),
                  pl.BlockSpec((nb, n1, lanes), lambda j, p: (j, 0, 0))],
        out_specs=pl.BlockSpec((None, 2, h1, nb * c), lambda j, p: (p, 0, 0, j)),
        out_shape=jax.ShapeDtypeStruct((pairs, 2, h1, w), BF16),
        scratch_shapes=[pltpu.VMEM((2 * n1, nb * c), BF16)],
        compiler_params=_cparams("arbitrary", "arbitrary"),
        name="fft3",
    )(s2, f1i, twr, twi)


def _hyena_long_conv(u, hf, hb, asum):
    b, l, c = u.shape
    n, n1, n2 = _fft_dims(l)
    h1 = n1 // 2
    f1f, f1i, f2f, f2i, twr, twi = _fft_tables(l, 128)
    zeros = jnp.zeros((l, c), BF16)
    taps = jnp.stack([hf.astype(BF16), zeros, hb.astype(BF16), zeros]).reshape(4, h1, n2 * c)
    s1k = _fft1(taps, f1f, twr, twi, c=c).reshape(2, 2, n1, n2, c)
    khat = _fft_kernel_spectrum(s1k, f2f, asum, n=n)
    s1 = _fft1(u.reshape(b, h1, n2 * c), f1f, twr, twi, c=c).reshape(b // 2, 2, n1, n2, c)
    s2 = _fft2(s1, khat, f2f, f2i).reshape(b // 2, 2 * n1, n2 * c)
    return _fft3(s2, f1i, twr, twi, c=c).reshape(b, l, c)


def _ret_kernel(q_ref, k_ref, v_ref, g_ref, cos_ref, sin_ref, pm_ref, lgf_ref, lgb_ref, sf0_ref, sb0_ref,
                o_ref, sf_ref, sb_ref, sbs_ref, *, nb, dk):
    s = pl.program_id(2)
    ch = RET_CHUNK
    nc = k_ref.shape[0] // ch
    lgf = lgf_ref[...]
    lgb = lgb_ref[...]
    ii = lax.broadcasted_iota(jnp.int32, (ch, dk), 0).astype(F32)
    scale = dk ** -0.5

    def rot(x_bf, r0):
        sw = _bdot(x_bf, pm_ref[...])
        return x_bf.astype(F32) * cos_ref[r0:r0 + ch, :] + sw * sin_ref[r0:r0 + ch, :]

    @pl.when(s == 0)
    def _():
        sb_ref[...] = sb0_ref[...]

    @pl.when(s == nb)
    def _():
        sf_ref[...] = sf0_ref[...]

    @pl.when(s < nb)
    def _():
        blk = nb - 1 - s
        zb = jnp.exp(lgb * ii)
        gch = jnp.exp(lgb * float(ch))
        for cc in reversed(range(nc)):
            r0 = cc * ch
            kc = rot(k_ref[r0:r0 + ch, :], r0) * scale
            sbs_ref[blk * nc + cc] = sb_ref[...]
            upd = _bdot((kc * zb).T.astype(BF16), v_ref[r0:r0 + ch, :])
            sb_ref[...] = sb_ref[...] * gch + upd

    @pl.when(s >= nb)
    def _():
        blk = s - nb
        ri = lax.broadcasted_iota(jnp.int32, (ch, ch), 0)
        ci = lax.broadcasted_iota(jnp.int32, (ch, ch), 1)
        rel = (ri - ci).astype(F32)
        dmat = jnp.where(ri > ci, jnp.exp(lgf * jnp.maximum(rel, 0.0)),
                         jnp.where(ri < ci, jnp.exp(lgb * jnp.maximum(-rel, 0.0)), 2.0))
        xf = jnp.exp(lgf * (ii + 1.0))
        xb = jnp.exp(lgb * (float(ch) - ii))
        zf = jnp.exp(lgf * (float(ch) - 1.0 - ii))
        gch = jnp.exp(lgf * float(ch))
        for cc in range(nc):
            r0 = cc * ch
            qc = rot(q_ref[r0:r0 + ch, :], r0)
            kc = rot(k_ref[r0:r0 + ch, :], r0) * scale
            vc = v_ref[r0:r0 + ch, :]
            sc = _bdot_nt(qc.astype(BF16), kc.astype(BF16)) * dmat
            qx = jnp.concatenate([qc * xf, qc * xb], axis=1).astype(BF16)
            st = jnp.concatenate([sf_ref[...], sbs_ref[blk * nc + cc]], axis=0).astype(BF16)
            o = _bdot(sc.astype(BF16), vc) + _bdot(qx, st)
            sf_ref[...] = sf_ref[...] * gch + _bdot((kc * zf).T.astype(BF16), vc)
            y = o * lax.rsqrt(jnp.mean(o * o, axis=-1, keepdims=True) + EPS)
            o_ref[r0:r0 + ch, :] = (y * _silu(g_ref[r0:r0 + ch, :].astype(F32))).astype(BF16)


def _retention(p, cosf, sinf, pm, lgf, lgb, sf0, sb0, *, q_off, k_off, v_off, g_off, heads, dk, dv, tb):
    b, l, _ = p.shape
    nb = l // tb
    fwd = lambda s: jnp.maximum(s - nb, 0)
    both = lambda s: jnp.where(s < nb, nb - 1 - s, s - nb)
    return pl.pallas_call(
        functools.partial(_ret_kernel, nb=nb, dk=dk),
        grid=(b, heads, 2 * nb),
        in_specs=[pl.BlockSpec((None, tb, dk), lambda i, h, s: (i, fwd(s), q_off // dk + h)),
                  pl.BlockSpec((None, tb, dk), lambda i, h, s: (i, both(s), k_off // dk + h)),
                  pl.BlockSpec((None, tb, dv), lambda i, h, s: (i, both(s), v_off // dv + h)),
                  pl.BlockSpec((None, tb, dv), lambda i, h, s: (i, fwd(s), g_off // dv + h)),
                  pl.BlockSpec((tb, dk), lambda i, h, s: (both(s), 0)),
                  pl.BlockSpec((tb, dk), lambda i, h, s: (both(s), 0)),
                  pl.BlockSpec((dk, dk), lambda i, h, s: (0, 0)),
                  pl.BlockSpec((None, 1, 1), lambda i, h, s: (h, 0, 0)),
                  pl.BlockSpec((None, 1, 1), lambda i, h, s: (h, 0, 0)),
                  pl.BlockSpec((None, None, dk, dv), lambda i, h, s: (i, h, 0, 0)),
                  pl.BlockSpec((None, None, dk, dv), lambda i, h, s: (i, h, 0, 0))],
        out_specs=pl.BlockSpec((None, tb, dv), lambda i, h, s: (i, fwd(s), h)),
        out_shape=jax.ShapeDtypeStruct((b, l, heads * dv), BF16),
        scratch_shapes=[pltpu.VMEM((dk, dv), F32), pltpu.VMEM((dk, dv), F32),
                        pltpu.VMEM((l // RET_CHUNK, dk, dv), F32)],
        compiler_params=_cparams("arbitrary", "arbitrary", "arbitrary"),
        name="ret",
    )(p, p, p, p, cosf, sinf, pm, lgf, lgb, sf0, sb0)


def _merge_kernel(y_ref, u_ref, x0_ref, r_ref, ah_ref, ar_ref, x_ref, g1_ref, sh_ref, sc_ref, n2_ref, hb_ref,
                  wh_ref, wr_ref, wo_ref, x2_ref, h2_ref):
    yh = (y_ref[...].astype(F32) + u_ref[...].astype(F32) * hb_ref[...]) * x0_ref[...].astype(F32)
    y_h = _bdot(yh.astype(BF16), wh_ref[...])
    y_r = _bdot(r_ref[...], wr_ref[...])
    m = _sigmoid(ah_ref[...].astype(F32)) * y_h + _sigmoid(ar_ref[...].astype(F32)) * y_r
    x2 = x_ref[...] + g1_ref[...] * _bdot(m.astype(BF16), wo_ref[...])
    x2_ref[...] = x2
    h2_ref[...] = (_rms(x2, n2_ref[...]) * (1.0 + sc_ref[...]) + sh_ref[...]).astype(BF16)


def _merge(y, u, x0, r, p, x, g1, sh2, sc2, norm2, hy_bias, wh, wr, wo, *, ah_off, ar_off, tm):
    b, l, d = x.shape
    c = y.shape[2]
    rw = r.shape[2]
    tile = lambda w, cb=0: pl.BlockSpec((None, tm, w), lambda i, j: (i, j, cb))
    vec = pl.BlockSpec((None, 1, d), lambda i, j: (i, 0, 0))
    return pl.pallas_call(
        _merge_kernel,
        grid=(b, l // tm),
        in_specs=[tile(c), tile(c), tile(c), tile(rw), tile(d, ah_off // d), tile(d, ar_off // d), tile(d),
                  vec, vec, vec,
                  pl.BlockSpec((1, d), lambda i, j: (0, 0)),
                  pl.BlockSpec((1, c), lambda i, j: (0, 0)),
                  _const_spec((c, d)), _const_spec((rw, d)), _const_spec((d, d))],
        out_specs=[tile(d), tile(d)],
        out_shape=[jax.ShapeDtypeStruct((b, l, d), F32), jax.ShapeDtypeStruct((b, l, d), BF16)],
        compiler_params=_cparams("arbitrary", "arbitrary"),
        name="merge",
    )(y, u, x0, r, p, p, x, g1, sh2, sc2, norm2, hy_bias, wh, wr, wo)


LOWEST_BITS = -0x00800001


def _rank_code(r):
    return np.array(LOWEST_BITS - r, np.int32).view(np.float32).item()


def _top_sorted(s, k):
    vals = []
    for r in range(k):
        m = jnp.max(s, axis=0, keepdims=True)
        vals.append(m)
        s = jnp.where(s == m, _rank_code(r), s)
    coded = s <= _rank_code(k - 1)
    rank = jnp.where(coded, LOWEST_BITS - pltpu.bitcast(s, jnp.int32), k).astype(F32)
    return vals, rank


def _pair_pack(v):
    b = pltpu.bitcast(v.astype(BF16).astype(F32), jnp.uint32)
    return b | (b >> 16)


def _row_bcast(row, rows):
    rep = pltpu.bitcast(jnp.broadcast_to(row, (8, row.shape[1])), BF16)
    return jnp.tile(rep, (rows // 16, 1))


def _peer_pairs(k):
    return [(i, j) for i in range(k) for j in range(k) if (i + 1) * (j + 1) <= k]


def _peer_kernel(h_ref, x_ref, g2_ref, fn_ref, wq_ref, kd_ref, u_ref, vt_ref, o_ref,
                 cn_ref, e1_ref, rk_ref, e2_ref, q_ref, s_ref, cand_ref, ht_ref, a_ref, w_ref, acc_ref,
                 *, topk, split):
    j = pl.program_id(2)
    nk = PEER_N_KEYS
    t = h_ref.shape[0]
    eb = u_ref.shape[0]
    pairs = _peer_pairs(topk)

    @pl.when(j == 0)
    def _():
        q = _bdot(h_ref[...], wq_ref[...]).astype(BF16)
        for hd in range(PEER_HEADS):
            q_ref[hd] = q[:, hd * PEER_DK:(hd + 1) * PEER_DK]
        cand_ref[...] = jnp.full(cand_ref.shape, NEG_INF, F32)

        def head(hd, carry):
            s_ref[...] = _bdot_nt(kd_ref[hd], q_ref[hd])
            for c0 in range(0, t, LANES):
                cols = slice(c0, c0 + LANES)
                s1, s2 = s_ref[0:nk, cols], s_ref[nk:, cols]
                a, rank1 = _top_sorted(s1, topk)
                b, rank2 = _top_sorted(s2, topk)
                for r, (pi, pj) in enumerate(pairs):
                    cand_ref[r:r + 1, :] = a[pi] + b[pj]
                cs = cand_ref[...]
                tau = _top_sorted(cs, topk)[0][-1]
                sel = cs >= tau
                mx = a[0] + b[0]
                z = jnp.sum(jnp.where(sel, jnp.exp(cs - mx), 0.0), axis=0, keepdims=True)
                self = jnp.where(sel, 1.0, 0.0)
                cnt1 = jnp.zeros(s1.shape, F32)
                for c in range(topk // 2):
                    height = sum(self[r:r + 1] for r, (_, pj) in enumerate(pairs) if pj == c)
                    cnt1 = jnp.where(rank1 < height, float(c + 1), cnt1)
                width0 = sum(self[r:r + 1] for r, (pi, _) in enumerate(pairs) if pi == 0)
                cnt1 = jnp.where(rank1 < 1.0, width0, cnt1)
                cn_ref[hd, :, cols] = _pair_pack(cnt1)
                e1_ref[hd, :, cols] = _pair_pack(jnp.exp(s1 - a[0]))
                rk_ref[hd, :, cols] = rank2.astype(BF16)
                e2_ref[hd, :, cols] = (jnp.exp(s2 - b[0]) / z).astype(BF16)
            return carry

        lax.fori_loop(0, PEER_HEADS, head, 0)
        acc_ref[...] = jnp.zeros_like(acc_ref)
        ht_ref[...] = h_ref[...].astype(F32).T.astype(BF16)

    part = eb // split
    zero = jnp.zeros((), BF16)

    for s in range(split):
        rows = slice(s * part, (s + 1) * part)
        a_ref[rows, :] = _bdot(u_ref[rows, :], ht_ref[...])
    for s in range(split):
        rows = slice(s * part, (s + 1) * part)
        for g in range(part // nk):
            r0 = s * part + g * nk
            i = j * (eb // nk) + r0 // nk
            act = _gelu_tanh(a_ref[r0:r0 + nk, :].astype(BF16))
            gate = jnp.zeros((nk, t), BF16)
            for hd in range(PEER_HEADS):
                cnt = _row_bcast(cn_ref[hd, pl.ds(i, 1), :], nk)
                e1 = _row_bcast(e1_ref[hd, pl.ds(i, 1), :], nk)
                gate = gate + jnp.where(rk_ref[hd] < cnt, e2_ref[hd], zero) * e1
            w_ref[r0:r0 + nk, :] = gate * act
        acc_ref[...] += _bdot(vt_ref[:, rows], w_ref[rows, :])

    @pl.when(j == pl.num_programs(2) - 1)
    def _():
        x3 = x_ref[...] + g2_ref[...] * acc_ref[...].T
        o_ref[...] = _rms(x3, fn_ref[...])


def _peer(h2, x2, g2, fnorm, wq, kd, u_bf, vt_bf, *, t, eb):
    b, l, d = x2.shape
    e = u_bf.shape[0]
    nk = PEER_N_KEYS
    npair = -(-len(_peer_pairs(PEER_TOPK)) // 8) * 8
    nj = e // eb
    tile =pl.BlockSpec((None, t, d), lambda i, m, j: (i, m, 0))
    return pl.pallas_call(
        functools.partial(_peer_kernel, topk=PEER_TOPK, split=4),
        grid=(b, l // t, nj),
        in_specs=[tile, tile,
                  pl.BlockSpec((None, 1, d), lambda i, m, j: (i, 0, 0)),
                  pl.BlockSpec((1, d), lambda i, m, j: (0, 0)),
                  _const_spec((d, PEER_HEADS * PEER_DK)),
                  _const_spec((PEER_HEADS, 2 * nk, PEER_DK)),
                  pl.BlockSpec((eb, d), lambda i, m, j: (j, 0)),
                  pl.BlockSpec((d, eb), lambda i, m, j: (0, j))],
        out_specs=tile,
        out_shape=jax.ShapeDtypeStruct((b, l, d), F32),
        scratch_shapes=[pltpu.VMEM((PEER_HEADS, nk, t), jnp.uint32), pltpu.VMEM((PEER_HEADS, nk, t), jnp.uint32),
                        pltpu.VMEM((PEER_HEADS, nk, t), BF16), pltpu.VMEM((PEER_HEADS, nk, t), BF16),
                        pltpu.VMEM((PEER_HEADS, t, PEER_DK), BF16),
                          pltpu.VMEM((2 * nk, t), F32),
                          pltpu.VMEM((npair, LANES), F32),
                          pltpu.VMEM((d, t), BF16),
                          pltpu.VMEM((eb, t), F32), pltpu.VMEM((eb, t), BF16),
                          pltpu.VMEM((d, t), F32)],
        compiler_params=_cparams("arbitrary", "arbitrary", "arbitrary"),
        name="peer",
    )(h2, x2, g2, fnorm, wq, kd, u_bf, vt_bf)


def _filter_features(l, emb):
    bands_n = (emb - 1) // 2
    t = jnp.linspace(0.0, 1.0, l, dtype=F32)[:, None]
    bands = jnp.linspace(1e-4, bands_n - 1, bands_n, dtype=F32)[None, :]
    w = (2.0 * math.pi / l) * jnp.arange(l, dtype=F32)[:, None]
    z = jnp.concatenate([t, jnp.cos(bands * w), -jnp.sin(bands * w)], axis=-1)
    return t, z


def _rotary_tables(l, dk):
    half = dk // 2
    nf = half // 2
    inv = ROPE_BASE ** (-jnp.arange(nf, dtype=F32) / nf)
    pos = jnp.arange(l, dtype=jnp.int32)
    rows = (pos // GRID_W).astype(F32)
    cols = (pos % GRID_W).astype(F32)
    ar = rows[:, None] * inv[None, :]
    ac = cols[:, None] * inv[None, :]
    cosf = jnp.concatenate([jnp.cos(ar), jnp.cos(ar), jnp.cos(ac), jnp.cos(ac)], axis=1)
    sinf = jnp.concatenate([-jnp.sin(ar), jnp.sin(ar), -jnp.sin(ac), jnp.sin(ac)], axis=1)
    lane = jnp.arange(dk, dtype=jnp.int32)
    pm = (lane[:, None] == (lane[None, :] ^ nf)).astype(BF16)
    return cosf, sinf, pm


def kernel(x, c, ctx, c_ctx, w_ada, b_ada, norm1, norm2, w_in, hy_conv_w, hy_conv_b, hy_fw1, hy_fb1, hy_fw2, hy_fb2, hy_fw3, hy_fb3, hy_fw4, hy_sin_freq, hy_deltas, hy_bias, ret_log_decay_f, ret_log_decay_b, w_hy_out, w_ret_out, w_o, peer_w_query, peer_sub_keys, peer_u, peer_v, final_norm):
    assert w_ada.shape[0] == 1, "single-layer configuration"
    b, l, d = x.shape
    assert b % 2 == 0 and l % RET_CHUNK == 0
    heads = RET_HEADS
    dk = d // 8
    dv = 2 * dk
    c_hy = d
    hy_cols = 3 * c_hy
    q_off = hy_cols
    k_off = q_off + heads * dk
    v_off = k_off + heads * dk
    g_off = v_off + heads * dv
    ah_off = g_off + heads * dv
    ar_off = ah_off + d

    rows = -(-(b + 1) // 8) * 8
    cc = jnp.zeros((rows, d), F32).at[:b].set(c).at[b].set(c_ctx)
    mod = _mod(cc, w_ada[0], b_ada[0][None, :])
    mod_l = mod[:b].reshape(b, 6, 1, d)
    sh1, sc1, g1, sh2, sc2, g2 = (mod_l[:, i] for i in range(6))
    mod_c = mod[b].reshape(6, 1, d)
    csh1, csc1 = mod_c[0], mod_c[1]

    w_in_bf = w_in[0].astype(BF16)
    n1g = norm1[0][None, :]
    lgf = ret_log_decay_f[0]
    lgb = ret_log_decay_b[0]

    sf0, sb0 = _ctx_states(ctx, csh1, csc1, n1g, w_in_bf, lgf[:, None], lgb[:, None],
                           k_off=k_off, v_off=v_off, heads=heads, dk=dk, dv=dv)

    p = _inproj(x, sh1, sc1, n1g, w_in_bf, tm=min(512, l))

    u, x0c = _hy_pre(p, hy_conv_w[0], hy_conv_b[0][None, :], c=c_hy, tl=min(256, l))
    emb = hy_fw1.shape[1]
    order = hy_fw1.shape[2]
    t_lin, z = _filter_features(l, emb)
    epad = -(-emb // 64) * 64
    z = jnp.pad(z, ((0, 0), (0, epad - emb)))
    fw1 = jnp.pad(hy_fw1[0], ((0, epad - emb), (0, 0)))
    hf, hb, asum = _hy_filter(z, t_lin, fw1, hy_fb1[0][None, :], hy_fw2[0], hy_fb2[0][None, :],
                              hy_fw3[0], hy_fb3[0][None, :], hy_fw4[0], hy_sin_freq[0][None, :],
                              hy_deltas[0][None, :], tl=min(1024, l))
    del order
    y = _hyena_long_conv(u, hf, hb, asum)

    cosf, sinf, pm = _rotary_tables(l, dk)
    r = _retention(p, cosf, sinf, pm, lgf[:, None, None], lgb[:, None, None], sf0, sb0,
                   q_off=q_off, k_off=k_off, v_off=v_off, g_off=g_off, heads=heads, dk=dk, dv=dv,
                   tb=min(1024, l))

    x2, h2 = _merge(y, u, x0c, r, p, x, g1, sh2, sc2, norm2[0][None, :], hy_bias[0][None, :],
                    w_hy_out[0].astype(BF16), w_ret_out[0].astype(BF16), w_o[0].astype(BF16),
                    ah_off=ah_off, ar_off=ar_off, tm=min(512, l))

    sk = peer_sub_keys[0]
    hk = sk.shape[3]
    zk = jnp.zeros_like(sk[:, 0])
    kd = jnp.concatenate([jnp.concatenate([sk[:, 0], zk], axis=2),
                          jnp.concatenate([zk, sk[:, 1]], axis=2)], axis=1).astype(BF16)
    del hk
    out = _peer(h2, x2, g2, final_norm[None, :], peer_w_query[0].astype(BF16), kd,
                peer_u[0].astype(BF16), peer_v[0].T.astype(BF16), t=min(512, l), eb=2048)
    return out
```

```python
import functools
import math

import jax
import jax.numpy as jnp
import numpy as np
from jax import lax
from jax.experimental import pallas as pl
from jax.experimental.pallas import tpu as pltpu

F32 = jnp.float32
BF16 = jnp.bfloat16
EPS = 1e-6

GRID_W = 64
RET_HEADS = 4
RET_CHUNK = 256
ROPE_BASE = 10000.0
PEER_HEADS = 8
PEER_N_KEYS = 128
PEER_TOPK = 16
PEER_DK = 128
HEAD_UNROLL = 4

VMEM_LIMIT_BYTES = 56 * 1024 * 1024
LANES = 128
MXU = 256
NEG_INF = float("-inf")


def _cparams(*sem):
    return pltpu.CompilerParams(dimension_semantics=sem, vmem_limit_bytes=VMEM_LIMIT_BYTES)


def _bdot(a, b):
    return jnp.dot(a, b, preferred_element_type=F32)


def _bdot_nt(a, b):
    return lax.dot_general(a, b, (((1,), (1,)), ((), ())), preferred_element_type=F32)


def _split(a):
    hi = a.astype(BF16)
    lo = (a - hi.astype(F32)).astype(BF16)
    return hi, lo


def _dot3(a, b):
    ah, al = _split(a)
    bh, bl = _split(b)
    return _bdot(ah, bh) + _bdot(ah, bl) + _bdot(al, bh)


def _sigmoid(x):
    return 1.0 / (1.0 + jnp.exp(-x))


def _silu(x):
    return x * _sigmoid(x)


def _gelu_tanh(x):
    k = 2.0 * math.sqrt(2.0 / math.pi) * math.log2(math.e)
    return x / (1.0 + jnp.exp2(-(x * (k + (k * 0.044715) * (x * x)))))


def _rms(x, gain):
    return x * lax.rsqrt(jnp.mean(x * x, axis=-1, keepdims=True) + EPS) * gain


def _const_spec(shape):
    nd = len(shape)
    return pl.BlockSpec(shape, lambda *_: (0,) * nd, pipeline_mode=pl.Buffered(1))


def _mod_kernel(c_ref, w_ref, b_ref, o_ref):
    o_ref[...] = _dot3(_silu(c_ref[...]), w_ref[...]) + b_ref[...]


def _mod(cc, w_ada, b_ada):
    rows, d = cc.shape
    n = w_ada.shape[1]
    return pl.pallas_call(
        _mod_kernel,
        grid=(n // d,),
        in_specs=[pl.BlockSpec((rows, d), lambda j: (0, 0)),
                  pl.BlockSpec((d, d), lambda j: (0, j)),
                  pl.BlockSpec((1, d), lambda j: (0, j))],
        out_specs=pl.BlockSpec((rows, d), lambda j: (0, j)),
        out_shape=jax.ShapeDtypeStruct((rows, n), F32),
        compiler_params=_cparams("arbitrary"),
        name="mod",
    )(cc, w_ada, b_ada)


def _ctx_kernel(ctx_ref, sh_ref, sc_ref, g_ref, wk_ref, wv_ref, lgf_ref, lgb_ref, sf_ref, sb_ref, *, heads, dk, dv):
    n = ctx_ref.shape[0]
    h = (_rms(ctx_ref[...], g_ref[...]) * (1.0 + sc_ref[...]) + sh_ref[...]).astype(BF16)
    kc = _bdot(h, wk_ref[...]) * (dk ** -0.5)
    vc = _bdot(h, wv_ref[...]).astype(BF16)
    pos = lax.broadcasted_iota(jnp.int32, (n, dk), 0).astype(F32)
    for hd in range(heads):
        kh = kc[:, hd * dk:(hd + 1) * dk]
        vh = vc[:, hd * dv:(hd + 1) * dv]
        wf = jnp.exp(lgf_ref[hd:hd + 1, :] * (n - 1.0 - pos))
        wb = jnp.exp(lgb_ref[hd:hd + 1, :] * pos)
        sf_ref[hd] = _bdot((kh * wf).T.astype(BF16), vh)
        sb_ref[hd] = _bdot((kh * wb).T.astype(BF16), vh)


def _ctx_states(ctx, csh, csc, gain, w_in_bf, lgf, lgb, *, k_off, v_off, heads, dk, dv):
    b, n, d = ctx.shape
    kw, vw = heads * dk, heads * dv
    out = jax.ShapeDtypeStruct((b, heads, dk, dv), F32)
    return pl.pallas_call(
        functools.partial(_ctx_kernel, heads=heads, dk=dk, dv=dv),
        grid=(b,),
        in_specs=[pl.BlockSpec((None, n, d), lambda i: (i, 0, 0)),
                  pl.BlockSpec((1, d), lambda i: (0, 0)),
                  pl.BlockSpec((1, d), lambda i: (0, 0)),
                  pl.BlockSpec((1, d), lambda i: (0, 0)),
                  pl.BlockSpec((d, kw), lambda i: (0, k_off // kw)),
                  pl.BlockSpec((d, vw), lambda i: (0, v_off // vw)),
                  pl.BlockSpec((heads, 1), lambda i: (0, 0)),
                  pl.BlockSpec((heads, 1), lambda i: (0, 0))],
        out_specs=[pl.BlockSpec((None, heads, dk, dv), lambda i: (i, 0, 0, 0)),
                   pl.BlockSpec((None, heads, dk, dv), lambda i: (i, 0, 0, 0))],
        out_shape=[out, out],
        compiler_params=_cparams("arbitrary"),
        name="ctx",
    )(ctx, csh, csc, gain, w_in_bf, w_in_bf, lgf, lgb)


def _inproj_kernel(x_ref, sh_ref, sc_ref, g_ref, w_ref, o_ref, *, nc):
    h = (_rms(x_ref[...], g_ref[...]) * (1.0 + sc_ref[...]) + sh_ref[...]).astype(BF16)
    n = w_ref.shape[1]
    for j in range(n // nc):
        o_ref[:, j * nc:(j + 1) * nc] = _bdot(h, w_ref[:, j * nc:(j + 1) * nc]).astype(BF16)


def _inproj(x, sh, sc, gain, w_bf, *, tm):
    b, l, d = x.shape
    n = w_bf.shape[1]
    return pl.pallas_call(
        functools.partial(_inproj_kernel, nc=1024),
        grid=(b, l // tm),
        in_specs=[pl.BlockSpec((None, tm, d), lambda i, j: (i, j, 0)),
                  pl.BlockSpec((None, 1, d), lambda i, j: (i, 0, 0)),
                  pl.BlockSpec((None, 1, d), lambda i, j: (i, 0, 0)),
                  pl.BlockSpec((1, d), lambda i, j: (0, 0)),
                  _const_spec((d, n))],
        out_specs=pl.BlockSpec((None, tm, n), lambda i, j: (i, j, 0)),
        out_shape=jax.ShapeDtypeStruct((b, l, n), BF16),
        compiler_params=_cparams("arbitrary", "arbitrary"),
        name="inproj",
    )(x, sh, sc, gain, w_bf)


HALO = 16


def _hy_pre_kernel(p_ref, pp_ref, pn_ref, w_ref, b_ref, u_ref, x0_ref, *, c, cw):
    i = pl.program_id(1)
    first = i == 0
    last = i == pl.num_programs(1) - 1
    tl = p_ref.shape[0]
    row = lax.broadcasted_iota(jnp.int32, (tl, cw), 0)
    for cb in range(c // cw):
        zs = []
        for part in range(3):
            c0 = part * c + cb * cw
            cur = p_ref[:, c0:c0 + cw].astype(F32)
            prev = pp_ref[:, c0:c0 + cw].astype(F32)[HALO - 1:HALO, :]
            nxt = pn_ref[:, c0:c0 + cw].astype(F32)[0:1, :]
            prev = jnp.where(first, 0.0, prev)
            nxt = jnp.where(last, 0.0, nxt)
            up = jnp.where(row == 0, prev, pltpu.roll(cur, 1, 0))
            dn = jnp.where(row == tl - 1, nxt, pltpu.roll(cur, tl - 1, 0))
            w = w_ref[:, c0:c0 + cw]
            zs.append(up * w[0:1, :] + cur * w[1:2, :] + dn * w[2:3, :] + b_ref[:, c0:c0 + cw])
        x0, x1, v = zs
        u_ref[:, cb * cw:(cb + 1) * cw] = (v * x1).astype(BF16)
        x0_ref[:, cb * cw:(cb + 1) * cw] = x0.astype(BF16)


def _hy_pre(p, conv_w, conv_b, *, c, tl):
    b, l, _ = p.shape
    hb = tl // HALO
    nh = l // HALO
    out = jax.ShapeDtypeStruct((b, l, c), BF16)
    return pl.pallas_call(
        functools.partial(_hy_pre_kernel, c=c, cw=256),
        grid=(b, l // tl),
        in_specs=[pl.BlockSpec((None, tl, 3 * c), lambda i, j: (i, j, 0)),
                  pl.BlockSpec((None, HALO, 3 * c), lambda i, j: (i, jnp.maximum(j * hb - 1, 0), 0)),
                  pl.BlockSpec((None, HALO, 3 * c), lambda i, j: (i, jnp.minimum((j + 1) * hb, nh - 1), 0)),
                  pl.BlockSpec((3, 3 * c), lambda i, j: (0, 0)),
                  pl.BlockSpec((1, 3 * c), lambda i, j: (0, 0))],
        out_specs=[pl.BlockSpec((None, tl, c), lambda i, j: (i, j, 0)),
                   pl.BlockSpec((None, tl, c), lambda i, j: (i, j, 0))],
        out_shape=[out, out],
        compiler_params=_cparams("arbitrary", "arbitrary"),
        name="hy_pre",
    )(p, p, p, conv_w, conv_b)


def _filt_kernel(z_ref, t_ref, w1, b1, w2, b2, w3, b3, w4, fr, dl, hf_ref, hb_ref, as_ref, *, c):
    i = pl.program_id(0)
    tl = z_ref.shape[0]
    f = fr[...]
    h = jnp.sin(f * (_dot3(z_ref[...], w1[...]) + b1[...]))
    h = jnp.sin(f * (_dot3(h, w2[...]) + b2[...]))
    h = jnp.sin(f * (_dot3(h, w3[...]) + b3[...]))
    h4 = _dot3(h, w4[...])
    win = jnp.exp(-t_ref[...] * jnp.abs(dl[...]))
    hf = h4[:, :c] * win
    row = lax.broadcasted_iota(jnp.int32, (tl, c), 0) + i * tl
    hb = jnp.where(row == 0, 0.0, h4[:, c:] * win)

    hf_ref[...] = hf
    hb_ref[...] = hb

    @pl.when(i == 0)
    def _():
        as_ref[...] = jnp.zeros_like(as_ref)

    as_ref[...] += jnp.sum(jnp.abs(hf) + jnp.abs(hb), axis=0, keepdims=True)


def _hy_filter(z, t, w1, b1, w2, b2, w3, b3, w4, fr, dl, *, tl):
    l, e = z.shape
    o = w2.shape[0]
    c = dl.shape[1]
    full = lambda shape: pl.BlockSpec(shape, lambda i: (0, 0))
    return pl.pallas_call(
        functools.partial(_filt_kernel, c=c),
        grid=(l // tl,),
        in_specs=[pl.BlockSpec((tl, e), lambda i: (i, 0)),
                  pl.BlockSpec((tl, 1), lambda i: (i, 0)),
                  full((e, o)), full((1, o)), full((o, o)), full((1, o)), full((o, o)), full((1, o)),
                  full((o, 2 * c)), full((1, o)), full((1, c))],
        out_specs=[pl.BlockSpec((tl, c), lambda i: (i, 0)),
                   pl.BlockSpec((tl, c), lambda i: (i, 0)),
                   pl.BlockSpec((1, c), lambda i: (0, 0))],
        out_shape=[jax.ShapeDtypeStruct((l, c), F32), jax.ShapeDtypeStruct((l, c), F32),
                   jax.ShapeDtypeStruct((1, c), F32)],
        compiler_params=_cparams("arbitrary"),
        name="filt",
    )(z, t, w1, b1, w2, b2, w3, b3, w4, fr, dl)


N2_BLOCK = 16
K1_BLOCK = 8


def _fft_dims(l):
    n = 2 * l
    n1 = math.isqrt(n)
    assert n1 * n1 == n and n1 % 32 == 0, "sequence length must make 2L a square of a multiple of 32"
    return n, n1, n1


def _fft_tables(l, lanes):
    n, n1, n2 = _fft_dims(l)
    h1 = n1 // 2
    k1 = jnp.arange(n1, dtype=jnp.int32)
    th1 = ((k1[:, None] * jnp.arange(h1, dtype=jnp.int32)[None, :]) % n1).astype(F32) * (2.0 * math.pi / n1)
    c1, s1 = jnp.cos(th1), jnp.sin(th1)
    f1f = jnp.concatenate([jnp.concatenate([c1, s1], axis=1), jnp.concatenate([-s1, c1], axis=1)], axis=0)
    f1i = f1f.T
    k2 = jnp.arange(n2, dtype=jnp.int32)
    th2 = ((k2[:, None] * k2[None, :]) % n2).astype(F32) * (2.0 * math.pi / n2)
    c2, s2 = jnp.cos(th2), jnp.sin(th2)
    f2f = jnp.concatenate([jnp.concatenate([c2, s2], axis=1), jnp.concatenate([-s2, c2], axis=1)], axis=0)
    f2i = jnp.concatenate([jnp.concatenate([c2, -s2], axis=1), jnp.concatenate([s2, c2], axis=1)], axis=0)
    tht = (k2[:, None] * k1[None, :]).astype(F32) * (2.0 * math.pi / n)
    twr = jnp.broadcast_to(jnp.cos(tht)[:, :, None], (n2, n1, lanes))
    twi = jnp.broadcast_to(jnp.sin(tht)[:, :, None], (n2, n1, lanes))
    return f1f.astype(BF16), f1i.astype(BF16), f2f.astype(BF16), f2i.astype(BF16), twr, twi


def _fft1_kernel(xr_ref, xi_ref, f_ref, twr_ref, twi_ref, o_ref, xs_ref, *, c):
    h1, nb = xr_ref.shape[0], xr_ref.shape[1]
    n1 = 2 * h1
    reps = c // twr_ref.shape[2]
    for ri, x_ref in enumerate((xr_ref, xi_ref)):
        xf = x_ref[...].reshape(h1 * nb, c).astype(F32)
        for cb in range(c // LANES):
            xs_ref[ri, cb] = xf[:, cb * LANES:(cb + 1) * LANES]
    for j in range(nb):
        x = jnp.concatenate(
            [jnp.concatenate([xs_ref[ri, cb, pl.ds(j, h1, stride=nb), :] for cb in range(c // LANES)], axis=1)
             for ri in range(2)], axis=0)
        a = _bdot(f_ref[...], x.astype(BF16))
        ar, ai = a[0:n1], a[n1:]
        wr = jnp.tile(twr_ref[j], (1, reps))
        wi = jnp.tile(twi_ref[j], (1, reps))
        o_ref[0:n1, j * c:(j + 1) * c] = (ar * wr + ai * wi).astype(BF16)
        o_ref[n1:, j * c:(j + 1) * c] = (ai * wr - ar * wi).astype(BF16)


def _fft1(u4, f1f, twr, twi):
    b, h1, n2, c = u4.shape
    n1 = 2 * h1
    nb = min(N2_BLOCK, n2)
    lanes = twr.shape[2]
    return pl.pallas_call(
        functools.partial(_fft1_kernel, c=c),
        grid=(n2 // nb, b // 2),
        in_specs=[pl.BlockSpec((None, h1, nb, c), lambda j, p: (2 * p, 0, j, 0)),
                  pl.BlockSpec((None, h1, nb, c), lambda j, p: (2 * p + 1, 0, j, 0)),
                  pl.BlockSpec((2 * n1, n1), lambda j, p: (0, 0)),
                  pl.BlockSpec((nb, n1, lanes), lambda j, p: (j, 0, 0)),
                  pl.BlockSpec((nb, n1, lanes), lambda j, p: (j, 0, 0))],
        out_specs=pl.BlockSpec((None, 2 * n1, nb * c), lambda j, p: (p, 0, j)),
        out_shape=jax.ShapeDtypeStruct((b // 2, 2 * n1, n2 * c), BF16),
        scratch_shapes=[pltpu.VMEM((2, c // LANES, h1 * nb, LANES), F32)],
        compiler_params=_cparams("arbitrary", "arbitrary"),
        name="fft1",
    )(u4, u4, f1f, twr, twi)


def _fftk_kernel(pr_ref, pi_ref, qr_ref, qi_ref, f_ref, as_ref, o_ref, *, n):
    n2 = pr_ref.shape[1]
    scale = 1.0 / (n * (as_ref[...] + EPS))
    for k in range(pr_ref.shape[0]):
        p = _bdot(f_ref[...], jnp.concatenate([pr_ref[k], pi_ref[k]], axis=0))
        q = _bdot(f_ref[...], jnp.concatenate([qr_ref[k], qi_ref[k]], axis=0))
        o_ref[k, 0:n2, :] = (p[0:n2] + q[0:n2]) * scale
        o_ref[k, n2:, :] = (p[n2:] - q[n2:]) * scale


def _fft_kernel_spectrum(s1, f2f, asum, *, n):
    _, _, n1, n2, c = s1.shape
    kb = K1_BLOCK
    spec = lambda pair, ri: pl.BlockSpec((None, None, kb, n2, c), lambda k: (pair, ri, k, 0, 0))
    return pl.pallas_call(
        functools.partial(_fftk_kernel, n=n),
        grid=(n1 // kb,),
        in_specs=[spec(0, 0), spec(0, 1), spec(1, 0), spec(1, 1),
                  pl.BlockSpec((2 * n2, 2 * n2), lambda k: (0, 0)),
                  pl.BlockSpec((1, c), lambda k: (0, 0))],
        out_specs=pl.BlockSpec((kb, 2 * n2, c), lambda k: (k, 0, 0)),
        out_shape=jax.ShapeDtypeStruct((n1, 2 * n2, c), F32),
        compiler_params=_cparams("arbitrary"),
        name="fftk",
    )(s1, s1, s1, s1, f2f, asum)


def _fft2_kernel(ar_ref, ai_ref, kh_ref, ff_ref, fi_ref, o_ref):
    n2 = ar_ref.shape[1]
    for k in range(ar_ref.shape[0]):
        x = _bdot(ff_ref[...], jnp.concatenate([ar_ref[k], ai_ref[k]], axis=0))
        xr, xi = x[0:n2], x[n2:]
        kr, ki = kh_ref[k, 0:n2, :], kh_ref[k, n2:, :]
        y = jnp.concatenate([xr * kr - xi * ki, xr * ki + xi * kr], axis=0).astype(BF16)
        b = _bdot(fi_ref[...], y)
        o_ref[0, k] = b[0:n2].astype(BF16)
        o_ref[1, k] = b[n2:].astype(BF16)


def _fft2(s1, khat, f2f, f2i):
    pairs, _, n1, n2, c = s1.shape
    kb = K1_BLOCK
    spec = lambda ri: pl.BlockSpec((None, None, kb, n2, c), lambda k, p: (p, ri, k, 0, 0))
    return pl.pallas_call(
        _fft2_kernel,
        grid=(n1 // kb, pairs),
        in_specs=[spec(0), spec(1),
                  pl.BlockSpec((kb, 2 * n2, c), lambda k, p: (k, 0, 0)),
                  pl.BlockSpec((2 * n2, 2 * n2), lambda k, p: (0, 0)),
                  pl.BlockSpec((2 * n2, 2 * n2), lambda k, p: (0, 0))],
        out_specs=pl.BlockSpec((None, 2, kb, n2, c), lambda k, p: (p, 0, k, 0, 0)),
        out_shape=jax.ShapeDtypeStruct((pairs, 2, n1, n2, c), BF16),
        compiler_params=_cparams("arbitrary", "arbitrary"),
        name="fft2",
    )(s1, s1, khat, f2f, f2i)


def _fft3_kernel(b_ref, f_ref, twr_ref, twi_ref, o_ref, bs_ref, ys_ref, *, c):
    n1 = twr_ref.shape[1]
    h1 = n1 // 2
    nb = twr_ref.shape[0]
    reps = c // twr_ref.shape[2]
    for ri in range(2):
        bf = b_ref[ri].reshape(n1 * nb, c).astype(F32)
        for cb in range(c // LANES):
            bs_ref[ri, cb] = bf[:, cb * LANES:(cb + 1) * LANES]
    for j in range(nb):
        br, bi = (jnp.concatenate([bs_ref[ri, cb, pl.ds(j, n1, stride=nb), :] for cb in range(c // LANES)], axis=1)
                  for ri in range(2))
        wr = jnp.tile(twr_ref[j], (1, reps))
        wi = jnp.tile(twi_ref[j], (1, reps))
        t = jnp.concatenate([br * wr - bi * wi, bi * wr + br * wi], axis=0).astype(BF16)
        y = _bdot(f_ref[...], t)
        for ri in range(2):
            for cb in range(c // LANES):
                ys_ref[ri, cb, pl.ds(j, h1, stride=nb), :] = y[ri * h1:(ri + 1) * h1, cb * LANES:(cb + 1) * LANES]
    for ri in range(2):
        yr = jnp.concatenate([ys_ref[ri, cb] for cb in range(c // LANES)], axis=1)
        o_ref[ri] = yr.astype(BF16).reshape(h1, nb, c)


def _fft3(s2, f1i, twr, twi):
    pairs, _, n1, n2, c = s2.shape
    h1 = n1 // 2
    nb = min(N2_BLOCK, n2)
    lanes = twr.shape[2]
    return pl.pallas_call(
        functools.partial(_fft3_kernel, c=c),
        grid=(n2 // nb, pairs),
        in_specs=[pl.BlockSpec((None, 2, n1, nb, c), lambda j, p: (p, 0, 0, j, 0)),
                  pl.BlockSpec((n1, 2 * n1), lambda j, p: (0, 0)),
                  pl.BlockSpec((nb, n1, lanes), lambda j, p: (j, 0, 0)),
                  pl.BlockSpec((nb, n1, lanes), lambda j, p: (j, 0, 0))],
        out_specs=pl.BlockSpec((None, 2, h1, nb, c), lambda j, p: (p, 0, 0, j, 0)),
        out_shape=jax.ShapeDtypeStruct((pairs, 2, h1, n2, c), BF16),
        scratch_shapes=[pltpu.VMEM((2, c // LANES, n1 * nb, LANES), F32),
                        pltpu.VMEM((2, c // LANES, h1 * nb, LANES), F32)],
        compiler_params=_cparams("arbitrary", "arbitrary"),
        name="fft3",
    )(s2, f1i, twr, twi)


def _hyena_long_conv(u, hf, hb, asum):
    b, l, c = u.shape
    n, n1, n2 = _fft_dims(l)
    h1 = n1 // 2
    f1f, f1i, f2f, f2i, twr, twi = _fft_tables(l, 128)
    zeros = jnp.zeros((l, c), BF16)
    taps = jnp.stack([hf.astype(BF16), zeros, hb.astype(BF16), zeros]).reshape(4, h1, n2, c)
    s1k = _fft1(taps, f1f, twr, twi).reshape(2, 2, n1, n2, c)
    khat = _fft_kernel_spectrum(s1k, f2f, asum, n=n)
    s1 = _fft1(u.reshape(b, h1, n2, c), f1f, twr, twi).reshape(b // 2, 2, n1, n2, c)
    s2 = _fft2(s1, khat, f2f, f2i)
    return _fft3(s2, f1i, twr, twi).reshape(b, l, c)


def _ret_kernel(q_ref, k_ref, v_ref, g_ref, cos_ref, sin_ref, pm_ref, lgf_ref, lgb_ref, sf0_ref, sb0_ref,
                o_ref, sf_ref, sb_ref, sbs_ref, *, nb, dk):
    s = pl.program_id(2)
    ch = RET_CHUNK
    nc = k_ref.shape[0] // ch
    lgf = lgf_ref[...]
    lgb = lgb_ref[...]
    ii = lax.broadcasted_iota(jnp.int32, (ch, dk), 0).astype(F32)
    scale = dk ** -0.5

    def rot(x_bf, r0):
        sw = _bdot(x_bf, pm_ref[...])
        return x_bf.astype(F32) * cos_ref[r0:r0 + ch, :] + sw * sin_ref[r0:r0 + ch, :]

    @pl.when(s == 0)
    def _():
        sb_ref[...] = sb0_ref[...]

    @pl.when(s == nb)
    def _():
        sf_ref[...] = sf0_ref[...]

    @pl.when(s < nb)
    def _():
        blk = nb - 1 - s
        zb = jnp.exp(lgb * ii)
        gch = jnp.exp(lgb * float(ch))
        for cc in reversed(range(nc)):
            r0 = cc * ch
            kc = rot(k_ref[r0:r0 + ch, :], r0) * scale
            sbs_ref[blk * nc + cc] = sb_ref[...]
            upd = _bdot((kc * zb).T.astype(BF16), v_ref[r0:r0 + ch, :])
            sb_ref[...] = sb_ref[...] * gch + upd

    @pl.when(s >= nb)
    def _():
        blk = s - nb
        ri = lax.broadcasted_iota(jnp.int32, (ch, ch), 0)
        ci = lax.broadcasted_iota(jnp.int32, (ch, ch), 1)
        rel = (ri - ci).astype(F32)
        dmat = jnp.where(ri > ci, jnp.exp(lgf * jnp.maximum(rel, 0.0)),
                         jnp.where(ri < ci, jnp.exp(lgb * jnp.maximum(-rel, 0.0)), 2.0))
        xf = jnp.exp(lgf * (ii + 1.0))
        xb = jnp.exp(lgb * (float(ch) - ii))
        zf = jnp.exp(lgf * (float(ch) - 1.0 - ii))
        gch = jnp.exp(lgf * float(ch))
        for cc in range(nc):
            r0 = cc * ch
            qc = rot(q_ref[r0:r0 + ch, :], r0)
            kc = rot(k_ref[r0:r0 + ch, :], r0) * scale
            vc = v_ref[r0:r0 + ch, :]
            sc = _bdot_nt(qc.astype(BF16), kc.astype(BF16)) * dmat
            qx = jnp.concatenate([qc * xf, qc * xb], axis=1).astype(BF16)
            st = jnp.concatenate([sf_ref[...], sbs_ref[blk * nc + cc]], axis=0).astype(BF16)
            o = _bdot(sc.astype(BF16), vc) + _bdot(qx, st)
            sf_ref[...] = sf_ref[...] * gch + _bdot((kc * zf).T.astype(BF16), vc)
            y = o * lax.rsqrt(jnp.mean(o * o, axis=-1, keepdims=True) + EPS)
            o_ref[r0:r0 + ch, :] = (y * _silu(g_ref[r0:r0 + ch, :].astype(F32))).astype(BF16)


def _retention(p, cosf, sinf, pm, lgf, lgb, sf0, sb0, *, q_off, k_off, v_off, g_off, heads, dk, dv, tb):
    b, l, _ = p.shape
    nb = l // tb
    fwd = lambda s: jnp.maximum(s - nb, 0)
    both = lambda s: jnp.where(s < nb, nb - 1 - s, s - nb)
    return pl.pallas_call(
        functools.partial(_ret_kernel, nb=nb, dk=dk),
        grid=(b, heads, 2 * nb),
        in_specs=[pl.BlockSpec((None, tb, dk), lambda i, h, s: (i, fwd(s), q_off // dk + h)),
                  pl.BlockSpec((None, tb, dk), lambda i, h, s: (i, both(s), k_off // dk + h)),
                  pl.BlockSpec((None, tb, dv), lambda i, h, s: (i, both(s), v_off // dv + h)),
                  pl.BlockSpec((None, tb, dv), lambda i, h, s: (i, fwd(s), g_off // dv + h)),
                  pl.BlockSpec((tb, dk), lambda i, h, s: (both(s), 0)),
                  pl.BlockSpec((tb, dk), lambda i, h, s: (both(s), 0)),
                  pl.BlockSpec((dk, dk), lambda i, h, s: (0, 0)),
                  pl.BlockSpec((None, 1, 1), lambda i, h, s: (h, 0, 0)),
                  pl.BlockSpec((None, 1, 1), lambda i, h, s: (h, 0, 0)),
                  pl.BlockSpec((None, None, dk, dv), lambda i, h, s: (i, h, 0, 0)),
                  pl.BlockSpec((None, None, dk, dv), lambda i, h, s: (i, h, 0, 0))],
        out_specs=pl.BlockSpec((None, tb, dv), lambda i, h, s: (i, fwd(s), h)),
        out_shape=jax.ShapeDtypeStruct((b, l, heads * dv), BF16),
        scratch_shapes=[pltpu.VMEM((dk, dv), F32), pltpu.VMEM((dk, dv), F32),
                        pltpu.VMEM((l // RET_CHUNK, dk, dv), F32)],
        compiler_params=_cparams("arbitrary", "arbitrary", "arbitrary"),
        name="ret",
    )(p, p, p, p, cosf, sinf, pm, lgf, lgb, sf0, sb0)


def _merge_kernel(y_ref, u_ref, x0_ref, r_ref, ah_ref, ar_ref, x_ref, g1_ref, sh_ref, sc_ref, n2_ref, hb_ref,
                  wh_ref, wr_ref, wo_ref, x2_ref, h2_ref):
    yh = (y_ref[...].astype(F32) + u_ref[...].astype(F32) * hb_ref[...]) * x0_ref[...].astype(F32)
    y_h = _bdot(yh.astype(BF16), wh_ref[...])
    y_r = _bdot(r_ref[...], wr_ref[...])
    m = _sigmoid(ah_ref[...].astype(F32)) * y_h + _sigmoid(ar_ref[...].astype(F32)) * y_r
    x2 = x_ref[...] + g1_ref[...] * _bdot(m.astype(BF16), wo_ref[...])
    x2_ref[...] = x2
    h2_ref[...] = (_rms(x2, n2_ref[...]) * (1.0 + sc_ref[...]) + sh_ref[...]).astype(BF16)


def _merge(y, u, x0, r, p, x, g1, sh2, sc2, norm2, hy_bias, wh, wr, wo, *, ah_off, ar_off, tm):
    b, l, d = x.shape
    c = y.shape[2]
    rw = r.shape[2]
    tile = lambda w, cb=0: pl.BlockSpec((None, tm, w), lambda i, j: (i, j, cb))
    vec = pl.BlockSpec((None, 1, d), lambda i, j: (i, 0, 0))
    return pl.pallas_call(
        _merge_kernel,
        grid=(b, l // tm),
        in_specs=[tile(c), tile(c), tile(c), tile(rw), tile(d, ah_off // d), tile(d, ar_off // d), tile(d),
                  vec, vec, vec,
                  pl.BlockSpec((1, d), lambda i, j: (0, 0)),
                  pl.BlockSpec((1, c), lambda i, j: (0, 0)),
                  _const_spec((c, d)), _const_spec((rw, d)), _const_spec((d, d))],
        out_specs=[tile(d), tile(d)],
        out_shape=[jax.ShapeDtypeStruct((b, l, d), F32), jax.ShapeDtypeStruct((b, l, d), BF16)],
        compiler_params=_cparams("arbitrary", "arbitrary"),
        name="merge",
    )(y, u, x0, r, p, p, x, g1, sh2, sc2, norm2, hy_bias, wh, wr, wo)


LOWEST_BITS = -0x00800001


def _rank_code(r):
    return np.array(LOWEST_BITS - r, np.int32).view(np.float32).item()


def _top_sorted(s, k):
    vals = []
    for r in range(k):
        m = jnp.max(s, axis=0, keepdims=True)
        vals.append(m)
        s = jnp.where(s == m, _rank_code(r), s)
    coded = s <= _rank_code(k - 1)
    rank = jnp.where(coded, LOWEST_BITS - pltpu.bitcast(s, jnp.int32), k).astype(F32)
    return vals, rank


def _pair_pack(v):
    b = pltpu.bitcast(v.astype(BF16).astype(F32), jnp.uint32)
    return b | (b >> 16)


def _row_bcast(row, rows):
    rep = pltpu.bitcast(jnp.broadcast_to(row, (8, row.shape[1])), BF16)
    return jnp.tile(rep, (rows // 16, 1))


def _peer_pairs(k):
    return [(i, j) for i in range(k) for j in range(k) if (i + 1) * (j + 1) <= k]


def _peer_kernel(h_ref, x_ref, g2_ref, fn_ref, wq_ref, kd_ref, u_ref, vt_ref, o_ref,
                 cn_ref, e1_ref, rk_ref, e2_ref, q_ref, s_ref, cand_ref, ht_ref, a_ref, w_ref, acc_ref,
                 *, topk, split):
    j = pl.program_id(2)
    nk = PEER_N_KEYS
    t = h_ref.shape[0]
    eb = u_ref.shape[0]
    pairs = _peer_pairs(topk)

    @pl.when(j == 0)
    def _():
        q = _bdot(h_ref[...], wq_ref[...]).astype(BF16)
        for hd in range(PEER_HEADS):
            q_ref[hd] = q[:, hd * PEER_DK:(hd + 1) * PEER_DK]
        cand_ref[...] = jnp.full(cand_ref.shape, NEG_INF, F32)

        def head(hd, s_ref, cand_ref):
            s_ref[...] = _bdot_nt(kd_ref[hd], q_ref[hd])
            for c0 in range(0, t, LANES):
                cols = slice(c0, c0 + LANES)
                s1, s2 = s_ref[0:nk, cols], s_ref[nk:, cols]
                a, rank1 = _top_sorted(s1, topk)
                b, rank2 = _top_sorted(s2, topk)
                for r, (pi, pj) in enumerate(pairs):
                    cand_ref[c0 // LANES, r:r + 1, :] = a[pi] + b[pj]
                cs = cand_ref[c0 // LANES]
                tau = _top_sorted(cs, topk)[0][-1]
                sel = cs >= tau
                mx = a[0] + b[0]
                z = jnp.sum(jnp.where(sel, jnp.exp(cs - mx), 0.0), axis=0, keepdims=True)
                self = jnp.where(sel, 1.0, 0.0)
                cnt1 = jnp.zeros(s1.shape, F32)
                for c in range(topk // 2):
                    height = sum(self[r:r + 1] for r, (_, pj) in enumerate(pairs) if pj == c)
                    cnt1 = jnp.where(rank1 < height, float(c + 1), cnt1)
                width0 = sum(self[r:r + 1] for r, (pi, _) in enumerate(pairs) if pi == 0)
                cnt1 = jnp.where(rank1 < 1.0, width0, cnt1)
                cn_ref[hd, :, cols] = _pair_pack(cnt1)
                e1_ref[hd, :, cols] = _pair_pack(jnp.exp(s1 - a[0]))
                rk_ref[hd, :, cols] = rank2.astype(BF16)
                e2_ref[hd, :, cols] = (jnp.exp(s2 - b[0]) / z).astype(BF16)

        def head_group(i, carry):
            for k in range(HEAD_UNROLL):
                head(i * HEAD_UNROLL + k, s_ref.at[k], cand_ref.at[k])
            return carry

        lax.fori_loop(0, PEER_HEADS // HEAD_UNROLL, head_group, 0)
        acc_ref[...] = jnp.zeros_like(acc_ref)
        ht_ref[...] = h_ref[...].astype(F32).T.astype(BF16)

    part = eb // split
    zero = jnp.zeros((), BF16)

    for s in range(split):
        rows = slice(s * part, (s + 1) * part)
        a_ref[rows, :] = _bdot(u_ref[rows, :], ht_ref[...])
    for s in range(split):
        rows = slice(s * part, (s + 1) * part)
        for g in range(part // nk):
            r0 = s * part + g * nk
            i = j * (eb // nk) + r0 // nk
            act = _gelu_tanh(a_ref[r0:r0 + nk, :].astype(BF16))
            gate = jnp.zeros((nk, t), BF16)
            for hd in range(PEER_HEADS):
                cnt = _row_bcast(cn_ref[hd, pl.ds(i, 1), :], nk)
                e1 = _row_bcast(e1_ref[hd, pl.ds(i, 1), :], nk)
                gate = gate + jnp.where(rk_ref[hd] < cnt, e2_ref[hd], zero) * e1
            w_ref[r0:r0 + nk, :] = gate * act
        acc_ref[...] += _bdot(vt_ref[:, rows], w_ref[rows, :])

    @pl.when(j == pl.num_programs(2) - 1)
    def _():
        x3 = x_ref[...] + g2_ref[...] * acc_ref[...].T
        o_ref[...] = _rms(x3, fn_ref[...])


def _peer(h2, x2, g2, fnorm, wq, kd, u_bf, vt_bf, *, t, eb):
    b, l, d = x2.shape
    e = u_bf.shape[0]
    nk = PEER_N_KEYS
    npair = -(-len(_peer_pairs(PEER_TOPK)) // 8) * 8
    nj = e // eb
    tile =pl.BlockSpec((None, t, d), lambda i, m, j: (i, m, 0))
    return pl.pallas_call(
        functools.partial(_peer_kernel, topk=PEER_TOPK, split=4),
        grid=(b, l // t, nj),
        in_specs=[tile, tile,
                  pl.BlockSpec((None, 1, d), lambda i, m, j: (i, 0, 0)),
                  pl.BlockSpec((1, d), lambda i, m, j: (0, 0)),
                  _const_spec((d, PEER_HEADS * PEER_DK)),
                  _const_spec((PEER_HEADS, 2 * nk, PEER_DK)),
                  pl.BlockSpec((eb, d), lambda i, m, j: (j, 0)),
                  pl.BlockSpec((d, eb), lambda i, m, j: (0, j))],
        out_specs=tile,
        out_shape=jax.ShapeDtypeStruct((b, l, d), F32),
        scratch_shapes=[pltpu.VMEM((PEER_HEADS, nk, t), jnp.uint32), pltpu.VMEM((PEER_HEADS, nk, t), jnp.uint32),
                        pltpu.VMEM((PEER_HEADS, nk, t), BF16), pltpu.VMEM((PEER_HEADS, nk, t), BF16),
                        pltpu.VMEM((PEER_HEADS, t, PEER_DK), BF16),
                          pltpu.VMEM((HEAD_UNROLL, 2 * nk, t), F32),
                          pltpu.VMEM((HEAD_UNROLL, t // LANES, npair, LANES), F32),
                          pltpu.VMEM((d, t), BF16),
                          pltpu.VMEM((eb, t), F32), pltpu.VMEM((eb, t), BF16),
                          pltpu.VMEM((d, t), F32)],
        compiler_params=_cparams("arbitrary", "arbitrary", "arbitrary"),
        name="peer",
    )(h2, x2, g2, fnorm, wq, kd, u_bf, vt_bf)


def _filter_features(l, emb):
    bands_n = (emb - 1) // 2
    t = jnp.linspace(0.0, 1.0, l, dtype=F32)[:, None]
    bands = jnp.linspace(1e-4, bands_n - 1, bands_n, dtype=F32)[None, :]
    w = (2.0 * math.pi / l) * jnp.arange(l, dtype=F32)[:, None]
    z = jnp.concatenate([t, jnp.cos(bands * w), -jnp.sin(bands * w)], axis=-1)
    return t, z


def _rotary_tables(l, dk):
    half = dk // 2
    nf = half // 2
    inv = ROPE_BASE ** (-jnp.arange(nf, dtype=F32) / nf)
    pos = jnp.arange(l, dtype=jnp.int32)
    rows = (pos // GRID_W).astype(F32)
    cols = (pos % GRID_W).astype(F32)
    ar = rows[:, None] * inv[None, :]
    ac = cols[:, None] * inv[None, :]
    cosf = jnp.concatenate([jnp.cos(ar), jnp.cos(ar), jnp.cos(ac), jnp.cos(ac)], axis=1)
    sinf = jnp.concatenate([-jnp.sin(ar), jnp.sin(ar), -jnp.sin(ac), jnp.sin(ac)], axis=1)
    lane = jnp.arange(dk, dtype=jnp.int32)
    pm = (lane[:, None] == (lane[None, :] ^ nf)).astype(BF16)
    return cosf, sinf, pm


def kernel(x, c, ctx, c_ctx, w_ada, b_ada, norm1, norm2, w_in, hy_conv_w, hy_conv_b, hy_fw1, hy_fb1, hy_fw2, hy_fb2, hy_fw3, hy_fb3, hy_fw4, hy_sin_freq, hy_deltas, hy_bias, ret_log_decay_f, ret_log_decay_b, w_hy_out, w_ret_out, w_o, peer_w_query, peer_sub_keys, peer_u, peer_v, final_norm):
    assert w_ada.shape[0] == 1, "single-layer configuration"
    b, l, d = x.shape
    assert b % 2 == 0 and l % RET_CHUNK == 0
    heads = RET_HEADS
    dk = d // 8
    dv = 2 * dk
    c_hy = d
    hy_cols = 3 * c_hy
    q_off = hy_cols
    k_off = q_off + heads * dk
    v_off = k_off + heads * dk
    g_off = v_off + heads * dv
    ah_off = g_off + heads * dv
    ar_off = ah_off + d

    rows = -(-(b + 1) // 8) * 8
    cc = jnp.zeros((rows, d), F32).at[:b].set(c).at[b].set(c_ctx)
    mod = _mod(cc, w_ada[0], b_ada[0][None, :])
    mod_l = mod[:b].reshape(b, 6, 1, d)
    sh1, sc1, g1, sh2, sc2, g2 = (mod_l[:, i] for i in range(6))
    mod_c = mod[b].reshape(6, 1, d)
    csh1, csc1 = mod_c[0], mod_c[1]

    w_in_bf = w_in[0].astype(BF16)
    n1g = norm1[0][None, :]
    lgf = ret_log_decay_f[0]
    lgb = ret_log_decay_b[0]

    sf0, sb0 = _ctx_states(ctx, csh1, csc1, n1g, w_in_bf, lgf[:, None], lgb[:, None],
                           k_off=k_off, v_off=v_off, heads=heads, dk=dk, dv=dv)

    p = _inproj(x, sh1, sc1, n1g, w_in_bf, tm=min(512, l))

    u, x0c = _hy_pre(p, hy_conv_w[0], hy_conv_b[0][None, :], c=c_hy, tl=min(1024, l))
    emb = hy_fw1.shape[1]
    order = hy_fw1.shape[2]
    t_lin, z = _filter_features(l, emb)
    epad = -(-emb // 64) * 64
    z = jnp.pad(z, ((0, 0), (0, epad - emb)))
    fw1 = jnp.pad(hy_fw1[0], ((0, epad - emb), (0, 0)))
    hf, hb, asum = _hy_filter(z, t_lin, fw1, hy_fb1[0][None, :], hy_fw2[0], hy_fb2[0][None, :],
                              hy_fw3[0], hy_fb3[0][None, :], hy_fw4[0], hy_sin_freq[0][None, :],
                              hy_deltas[0][None, :], tl=min(1024, l))
    del order
    y = _hyena_long_conv(u, hf, hb, asum)

    cosf, sinf, pm = _rotary_tables(l, dk)
    r = _retention(p, cosf, sinf, pm, lgf[:, None, None], lgb[:, None, None], sf0, sb0,
                   q_off=q_off, k_off=k_off, v_off=v_off, g_off=g_off, heads=heads, dk=dk, dv=dv,
                   tb=min(4096, l))

    x2, h2 = _merge(y, u, x0c, r, p, x, g1, sh2, sc2, norm2[0][None, :], hy_bias[0][None, :],
                    w_hy_out[0].astype(BF16), w_ret_out[0].astype(BF16), w_o[0].astype(BF16),
                    ah_off=ah_off, ar_off=ar_off, tm=min(512, l))

    sk = peer_sub_keys[0]
    hk = sk.shape[3]
    zk = jnp.zeros_like(sk[:, 0])
    kd = jnp.concatenate([jnp.concatenate([sk[:, 0], zk], axis=2),
                          jnp.concatenate([zk, sk[:, 1]], axis=2)], axis=1).astype(BF16)
    del hk
    out = _peer(h2, x2, g2, final_norm[None, :], peer_w_query[0].astype(BF16), kd,
                peer_u[0].astype(BF16), peer_v[0].T.astype(BF16), t=min(512, l), eb=2048)
    return out
```

```python
import functools
import math

import jax
import jax.numpy as jnp
import numpy as np
from jax import lax
from jax.experimental import pallas as pl
from jax.experimental.pallas import tpu as pltpu

F32 = jnp.float32
BF16 = jnp.bfloat16
EPS = 1e-6

GRID_W = 64
RET_HEADS = 4
RET_CHUNK = 256
ROPE_BASE = 10000.0
PEER_HEADS = 8
PEER_N_KEYS = 128
PEER_TOPK = 16
PEER_DK = 128
HEAD_UNROLL = 4

VMEM_LIMIT_BYTES = 56 * 1024 * 1024
LANES = 128
MXU = 256
NEG_INF = float("-inf")


def _cparams(*sem):
    return pltpu.CompilerParams(dimension_semantics=sem, vmem_limit_bytes=VMEM_LIMIT_BYTES)


def _bdot(a, b):
    return jnp.dot(a, b, preferred_element_type=F32)


def _bdot_nt(a, b):
    return lax.dot_general(a, b, (((1,), (1,)), ((), ())), preferred_element_type=F32)


def _split(a):
    hi = a.astype(BF16)
    lo = (a - hi.astype(F32)).astype(BF16)
    return hi, lo


def _dot3(a, b):
    ah, al = _split(a)
    bh, bl = _split(b)
    return _bdot(ah, bh) + _bdot(ah, bl) + _bdot(al, bh)


def _sigmoid(x):
    return 1.0 / (1.0 + jnp.exp(-x))


def _silu(x):
    return x * _sigmoid(x)


def _gelu_tanh(x):
    k = 2.0 * math.sqrt(2.0 / math.pi) * math.log2(math.e)
    return x / (1.0 + jnp.exp2(-(x * (k + (k * 0.044715) * (x * x)))))


def _rms(x, gain):
    return x * lax.rsqrt(jnp.mean(x * x, axis=-1, keepdims=True) + EPS) * gain


def _const_spec(shape):
    nd = len(shape)
    return pl.BlockSpec(shape, lambda *_: (0,) * nd, pipeline_mode=pl.Buffered(1))


def _mod_kernel(c_ref, w_ref, b_ref, o_ref):
    o_ref[...] = _dot3(_silu(c_ref[...]), w_ref[...]) + b_ref[...]


def _mod(cc, w_ada, b_ada):
    rows, d = cc.shape
    n = w_ada.shape[1]
    return pl.pallas_call(
        _mod_kernel,
        grid=(n // d,),
        in_specs=[pl.BlockSpec((rows, d), lambda j: (0, 0)),
                  pl.BlockSpec((d, d), lambda j: (0, j)),
                  pl.BlockSpec((1, d), lambda j: (0, j))],
        out_specs=pl.BlockSpec((rows, d), lambda j: (0, j)),
        out_shape=jax.ShapeDtypeStruct((rows, n), F32),
        compiler_params=_cparams("arbitrary"),
        name="mod",
    )(cc, w_ada, b_ada)


def _ctx_kernel(ctx_ref, sh_ref, sc_ref, g_ref, wk_ref, wv_ref, lgf_ref, lgb_ref, sf_ref, sb_ref, *, heads, dk, dv):
    n = ctx_ref.shape[0]
    h = (_rms(ctx_ref[...], g_ref[...]) * (1.0 + sc_ref[...]) + sh_ref[...]).astype(BF16)
    kc = _bdot(h, wk_ref[...]) * (dk ** -0.5)
    vc = _bdot(h, wv_ref[...]).astype(BF16)
    pos = lax.broadcasted_iota(jnp.int32, (n, dk), 0).astype(F32)
    for hd in range(heads):
        kh = kc[:, hd * dk:(hd + 1) * dk]
        vh = vc[:, hd * dv:(hd + 1) * dv]
        wf = jnp.exp(lgf_ref[hd:hd + 1, :] * (n - 1.0 - pos))
        wb = jnp.exp(lgb_ref[hd:hd + 1, :] * pos)
        sf_ref[hd] = _bdot((kh * wf).T.astype(BF16), vh)
        sb_ref[hd] = _bdot((kh * wb).T.astype(BF16), vh)


def _ctx_states(ctx, csh, csc, gain, w_in_bf, lgf, lgb, *, k_off, v_off, heads, dk, dv):
    b, n, d = ctx.shape
    kw, vw = heads * dk, heads * dv
    out = jax.ShapeDtypeStruct((b, heads, dk, dv), F32)
    return pl.pallas_call(
        functools.partial(_ctx_kernel, heads=heads, dk=dk, dv=dv),
        grid=(b,),
        in_specs=[pl.BlockSpec((None, n, d), lambda i: (i, 0, 0)),
                  pl.BlockSpec((1, d), lambda i: (0, 0)),
                  pl.BlockSpec((1, d), lambda i: (0, 0)),
                  pl.BlockSpec((1, d), lambda i: (0, 0)),
                  pl.BlockSpec((d, kw), lambda i: (0, k_off // kw)),
                  pl.BlockSpec((d, vw), lambda i: (0, v_off // vw)),
                  pl.BlockSpec((heads, 1), lambda i: (0, 0)),
                  pl.BlockSpec((heads, 1), lambda i: (0, 0))],
        out_specs=[pl.BlockSpec((None, heads, dk, dv), lambda i: (i, 0, 0, 0)),
                   pl.BlockSpec((None, heads, dk, dv), lambda i: (i, 0, 0, 0))],
        out_shape=[out, out],
        compiler_params=_cparams("arbitrary"),
        name="ctx",
    )(ctx, csh, csc, gain, w_in_bf, w_in_bf, lgf, lgb)


def _inproj_kernel(x_ref, sh_ref, sc_ref, g_ref, w_ref, o_ref, *, nc):
    h = (_rms(x_ref[...], g_ref[...]) * (1.0 + sc_ref[...]) + sh_ref[...]).astype(BF16)
    n = w_ref.shape[1]
    for j in range(n // nc):
        o_ref[:, j * nc:(j + 1) * nc] = _bdot(h, w_ref[:, j * nc:(j + 1) * nc]).astype(BF16)


def _inproj(x, sh, sc, gain, w_bf, *, tm):
    b, l, d = x.shape
    n = w_bf.shape[1]
    return pl.pallas_call(
        functools.partial(_inproj_kernel, nc=1024),
        grid=(b, l // tm),
        in_specs=[pl.BlockSpec((None, tm, d), lambda i, j: (i, j, 0)),
                  pl.BlockSpec((None, 1, d), lambda i, j: (i, 0, 0)),
                  pl.BlockSpec((None, 1, d), lambda i, j: (i, 0, 0)),
                  pl.BlockSpec((1, d), lambda i, j: (0, 0)),
                  _const_spec((d, n))],
        out_specs=pl.BlockSpec((None, tm, n), lambda i, j: (i, j, 0)),
        out_shape=jax.ShapeDtypeStruct((b, l, n), BF16),
        compiler_params=_cparams("arbitrary", "arbitrary"),
        name="inproj",
    )(x, sh, sc, gain, w_bf)


HALO = 16


def _hy_pre_kernel(p_ref, pp_ref, pn_ref, w_ref, b_ref, u_ref, x0_ref, *, c, cw):
    i = pl.program_id(1)
    first = i == 0
    last = i == pl.num_programs(1) - 1
    tl = p_ref.shape[0]
    row = lax.broadcasted_iota(jnp.int32, (tl, cw), 0)
    for cb in range(c // cw):
        zs = []
        for part in range(3):
            c0 = part * c + cb * cw
            cur = p_ref[:, c0:c0 + cw].astype(F32)
            prev = pp_ref[:, c0:c0 + cw].astype(F32)[HALO - 1:HALO, :]
            nxt = pn_ref[:, c0:c0 + cw].astype(F32)[0:1, :]
            prev = jnp.where(first, 0.0, prev)
            nxt = jnp.where(last, 0.0, nxt)
            up = jnp.where(row == 0, prev, pltpu.roll(cur, 1, 0))
            dn = jnp.where(row == tl - 1, nxt, pltpu.roll(cur, tl - 1, 0))
            w = w_ref[:, c0:c0 + cw]
            zs.append(up * w[0:1, :] + cur * w[1:2, :] + dn * w[2:3, :] + b_ref[:, c0:c0 + cw])
        x0, x1, v = zs
        u_ref[:, cb * cw:(cb + 1) * cw] = (v * x1).astype(BF16)
        x0_ref[:, cb * cw:(cb + 1) * cw] = x0.astype(BF16)


def _hy_pre(p, conv_w, conv_b, *, c, tl):
    b, l, _ = p.shape
    hb = tl // HALO
    nh = l // HALO
    out = jax.ShapeDtypeStruct((b, l, c), BF16)
    return pl.pallas_call(
        functools.partial(_hy_pre_kernel, c=c, cw=256),
        grid=(b, l // tl),
        in_specs=[pl.BlockSpec((None, tl, 3 * c), lambda i, j: (i, j, 0)),
                  pl.BlockSpec((None, HALO, 3 * c), lambda i, j: (i, jnp.maximum(j * hb - 1, 0), 0)),
                  pl.BlockSpec((None, HALO, 3 * c), lambda i, j: (i, jnp.minimum((j + 1) * hb, nh - 1), 0)),
                  pl.BlockSpec((3, 3 * c), lambda i, j: (0, 0)),
                  pl.BlockSpec((1, 3 * c), lambda i, j: (0, 0))],
        out_specs=[pl.BlockSpec((None, tl, c), lambda i, j: (i, j, 0)),
                   pl.BlockSpec((None, tl, c), lambda i, j: (i, j, 0))],
        out_shape=[out, out],
        compiler_params=_cparams("arbitrary", "arbitrary"),
        name="hy_pre",
    )(p, p, p, conv_w, conv_b)


def _filt_kernel(z_ref, t_ref, w1, b1, w2, b2, w3, b3, w4, fr, dl, hf_ref, hb_ref, as_ref, *, c):
    i = pl.program_id(0)
    tl = z_ref.shape[0]
    f = fr[...]
    h = jnp.sin(f * (_dot3(z_ref[...], w1[...]) + b1[...]))
    h = jnp.sin(f * (_dot3(h, w2[...]) + b2[...]))
    h = jnp.sin(f * (_dot3(h, w3[...]) + b3[...]))
    h4 = _dot3(h, w4[...])
    win = jnp.exp(-t_ref[...] * jnp.abs(dl[...]))
    hf = h4[:, :c] * win
    row = lax.broadcasted_iota(jnp.int32, (tl, c), 0) + i * tl
    hb = jnp.where(row == 0, 0.0, h4[:, c:] * win)

    hf_ref[...] = hf
    hb_ref[...] = hb

    @pl.when(i == 0)
    def _():
        as_ref[...] = jnp.zeros_like(as_ref)

    as_ref[...] += jnp.sum(jnp.abs(hf) + jnp.abs(hb), axis=0, keepdims=True)


def _hy_filter(z, t, w1, b1, w2, b2, w3, b3, w4, fr, dl, *, tl):
    l, e = z.shape
    o = w2.shape[0]
    c = dl.shape[1]
    full = lambda shape: pl.BlockSpec(shape, lambda i: (0, 0))
    return pl.pallas_call(
        functools.partial(_filt_kernel, c=c),
        grid=(l // tl,),
        in_specs=[pl.BlockSpec((tl, e), lambda i: (i, 0)),
                  pl.BlockSpec((tl, 1), lambda i: (i, 0)),
                  full((e, o)), full((1, o)), full((o, o)), full((1, o)), full((o, o)), full((1, o)),
                  full((o, 2 * c)), full((1, o)), full((1, c))],
        out_specs=[pl.BlockSpec((tl, c), lambda i: (i, 0)),
                   pl.BlockSpec((tl, c), lambda i: (i, 0)),
                   pl.BlockSpec((1, c), lambda i: (0, 0))],
        out_shape=[jax.ShapeDtypeStruct((l, c), F32), jax.ShapeDtypeStruct((l, c), F32),
                   jax.ShapeDtypeStruct((1, c), F32)],
        compiler_params=_cparams("arbitrary"),
        name="filt",
    )(z, t, w1, b1, w2, b2, w3, b3, w4, fr, dl)


N2_BLOCK = 16
K1_BLOCK = 8


def _fft_dims(l):
    n = 2 * l
    n1 = math.isqrt(n)
    assert n1 * n1 == n and n1 % 32 == 0, "sequence length must make 2L a square of a multiple of 32"
    return n, n1, n1


def _fft_tables(l, lanes):
    n, n1, n2 = _fft_dims(l)
    h1 = n1 // 2
    k1 = jnp.arange(n1, dtype=jnp.int32)
    th1 = ((k1[:, None] * jnp.arange(h1, dtype=jnp.int32)[None, :]) % n1).astype(F32) * (2.0 * math.pi / n1)
    c1, s1 = jnp.cos(th1), jnp.sin(th1)
    f1f = jnp.concatenate([jnp.concatenate([c1, s1], axis=1), jnp.concatenate([-s1, c1], axis=1)], axis=0)
    f1i = f1f.T
    k2 = jnp.arange(n2, dtype=jnp.int32)
    th2 = ((k2[:, None] * k2[None, :]) % n2).astype(F32) * (2.0 * math.pi / n2)
    c2, s2 = jnp.cos(th2), jnp.sin(th2)
    f2f = jnp.concatenate([jnp.concatenate([c2, s2], axis=1), jnp.concatenate([-s2, c2], axis=1)], axis=0)
    f2i = jnp.concatenate([jnp.concatenate([c2, -s2], axis=1), jnp.concatenate([s2, c2], axis=1)], axis=0)
    tht = (k2[:, None] * k1[None, :]).astype(F32) * (2.0 * math.pi / n)
    twr = jnp.broadcast_to(jnp.cos(tht)[:, :, None], (n2, n1, lanes))
    twi = jnp.broadcast_to(jnp.sin(tht)[:, :, None], (n2, n1, lanes))
    return f1f.astype(BF16), f1i.astype(BF16), f2f.astype(BF16), f2i.astype(BF16), twr, twi


def _fft1_kernel(xr_ref, xi_ref, f_ref, twr_ref, twi_ref, o_ref, xs_ref, *, c):
    h1, nb = xr_ref.shape[0], xr_ref.shape[1]
    n1 = 2 * h1
    reps = c // twr_ref.shape[2]
    for ri, x_ref in enumerate((xr_ref, xi_ref)):
        xf = x_ref[...].reshape(h1 * nb, c).astype(F32)
        for cb in range(c // LANES):
            xs_ref[ri, cb] = xf[:, cb * LANES:(cb + 1) * LANES]
    for j in range(nb):
        x = jnp.concatenate(
            [jnp.concatenate([xs_ref[ri, cb, pl.ds(j, h1, stride=nb), :] for cb in range(c // LANES)], axis=1)
             for ri in range(2)], axis=0)
        a = _bdot(f_ref[...], x.astype(BF16))
        ar, ai = a[0:n1], a[n1:]
        wr = jnp.tile(twr_ref[j], (1, reps))
        wi = jnp.tile(twi_ref[j], (1, reps))
        o_ref[0:n1, j * c:(j + 1) * c] = (ar * wr + ai * wi).astype(BF16)
        o_ref[n1:, j * c:(j + 1) * c] = (ai * wr - ar * wi).astype(BF16)


def _fft1(u4, f1f, twr, twi):
    b, h1, n2, c = u4.shape
    n1 = 2 * h1
    nb = min(N2_BLOCK, n2)
    lanes = twr.shape[2]
    return pl.pallas_call(
        functools.partial(_fft1_kernel, c=c),
        grid=(n2 // nb, b // 2),
        in_specs=[pl.BlockSpec((None, h1, nb, c), lambda j, p: (2 * p, 0, j, 0)),
                  pl.BlockSpec((None, h1, nb, c), lambda j, p: (2 * p + 1, 0, j, 0)),
                  pl.BlockSpec((2 * n1, n1), lambda j, p: (0, 0)),
                  pl.BlockSpec((nb, n1, lanes), lambda j, p: (j, 0, 0)),
                  pl.BlockSpec((nb, n1, lanes), lambda j, p: (j, 0, 0))],
        out_specs=pl.BlockSpec((None, 2 * n1, nb * c), lambda j, p: (p, 0, j)),
        out_shape=jax.ShapeDtypeStruct((b // 2, 2 * n1, n2 * c), BF16),
        scratch_shapes=[pltpu.VMEM((2, c // LANES, h1 * nb, LANES), F32)],
        compiler_params=_cparams("arbitrary", "arbitrary"),
        name="fft1",
    )(u4, u4, f1f, twr, twi)


def _fftk_kernel(pr_ref, pi_ref, qr_ref, qi_ref, f_ref, as_ref, o_ref, *, n):
    n2 = pr_ref.shape[1]
    scale = 1.0 / (n * (as_ref[...] + EPS))
    for k in range(pr_ref.shape[0]):
        p = _bdot(f_ref[...], jnp.concatenate([pr_ref[k], pi_ref[k]], axis=0))
        q = _bdot(f_ref[...], jnp.concatenate([qr_ref[k], qi_ref[k]], axis=0))
        o_ref[k, 0:n2, :] = (p[0:n2] + q[0:n2]) * scale
        o_ref[k, n2:, :] = (p[n2:] - q[n2:]) * scale


def _fft_kernel_spectrum(s1, f2f, asum, *, n):
    _, _, n1, n2, c = s1.shape
    kb = K1_BLOCK
    spec = lambda pair, ri: pl.BlockSpec((None, None, kb, n2, c), lambda k: (pair, ri, k, 0, 0))
    return pl.pallas_call(
        functools.partial(_fftk_kernel, n=n),
        grid=(n1 // kb,),
        in_specs=[spec(0, 0), spec(0, 1), spec(1, 0), spec(1, 1),
                  pl.BlockSpec((2 * n2, 2 * n2), lambda k: (0, 0)),
                  pl.BlockSpec((1, c), lambda k: (0, 0))],
        out_specs=pl.BlockSpec((kb, 2 * n2, c), lambda k: (k, 0, 0)),
        out_shape=jax.ShapeDtypeStruct((n1, 2 * n2, c), F32),
        compiler_params=_cparams("arbitrary"),
        name="fftk",
    )(s1, s1, s1, s1, f2f, asum)


def _fft2_kernel(ar_ref, ai_ref, kh_ref, ff_ref, fi_ref, o_ref):
    n2 = ar_ref.shape[1]
    for k in range(ar_ref.shape[0]):
        x = _bdot(ff_ref[...], jnp.concatenate([ar_ref[k], ai_ref[k]], axis=0))
        xr, xi = x[0:n2], x[n2:]
        kr, ki = kh_ref[k, 0:n2, :], kh_ref[k, n2:, :]
        y = jnp.concatenate([xr * kr - xi * ki, xr * ki + xi * kr], axis=0).astype(BF16)
        b = _bdot(fi_ref[...], y)
        o_ref[0, k] = b[0:n2].astype(BF16)
        o_ref[1, k] = b[n2:].astype(BF16)


def _fft2(s1, khat, f2f, f2i):
    pairs, _, n1, n2, c = s1.shape
    kb = K1_BLOCK
    spec = lambda ri: pl.BlockSpec((None, None, kb, n2, c), lambda k, p: (p, ri, k, 0, 0))
    return pl.pallas_call(
        _fft2_kernel,
        grid=(n1 // kb, pairs),
        in_specs=[spec(0), spec(1),
                  pl.BlockSpec((kb, 2 * n2, c), lambda k, p: (k, 0, 0)),
                  pl.BlockSpec((2 * n2, 2 * n2), lambda k, p: (0, 0)),
                  pl.BlockSpec((2 * n2, 2 * n2), lambda k, p: (0, 0))],
        out_specs=pl.BlockSpec((None, 2, kb, n2, c), lambda k, p: (p, 0, k, 0, 0)),
        out_shape=jax.ShapeDtypeStruct((pairs, 2, n1, n2, c), BF16),
        compiler_params=_cparams("arbitrary", "arbitrary"),
        name="fft2",
    )(s1, s1, khat, f2f, f2i)


def _fft3_kernel(b_ref, f_ref, twr_ref, twi_ref, o_ref, bs_ref, ys_ref, *, c):
    n1 = twr_ref.shape[1]
    h1 = n1 // 2
    nb = twr_ref.shape[0]
    reps = c // twr_ref.shape[2]
    for ri in range(2):
        bf = b_ref[ri].reshape(n1 * nb, c).astype(F32)
        for cb in range(c // LANES):
            bs_ref[ri, cb] = bf[:, cb * LANES:(cb + 1) * LANES]
    for j in range(nb):
        br, bi = (jnp.concatenate([bs_ref[ri, cb, pl.ds(j, n1, stride=nb), :] for cb in range(c // LANES)], axis=1)
                  for ri in range(2))
        wr = jnp.tile(twr_ref[j], (1, reps))
        wi = jnp.tile(twi_ref[j], (1, reps))
        t = jnp.concatenate([br * wr - bi * wi, bi * wr + br * wi], axis=0).astype(BF16)
        y = _bdot(f_ref[...], t)
        for ri in range(2):
            for cb in range(c // LANES):
                ys_ref[ri, cb, pl.ds(j, h1, stride=nb), :] = y[ri * h1:(ri + 1) * h1, cb * LANES:(cb + 1) * LANES]
    for ri in range(2):
        yr = jnp.concatenate([ys_ref[ri, cb] for cb in range(c // LANES)], axis=1)
        o_ref[ri] = yr.astype(BF16).reshape(h1, nb, c)


def _fft3(s2, f1i, twr, twi):
    pairs, _, n1, n2, c = s2.shape
    h1 = n1 // 2
    nb = min(N2_BLOCK, n2)
    lanes = twr.shape[2]
    return pl.pallas_call(
        functools.partial(_fft3_kernel, c=c),
        grid=(n2 // nb, pairs),
        in_specs=[pl.BlockSpec((None, 2, n1, nb, c), lambda j, p: (p, 0, 0, j, 0)),
                  pl.BlockSpec((n1, 2 * n1), lambda j, p: (0, 0)),
                  pl.BlockSpec((nb, n1, lanes), lambda j, p: (j, 0, 0)),
                  pl.BlockSpec((nb, n1, lanes), lambda j, p: (j, 0, 0))],
        out_specs=pl.BlockSpec((None, 2, h1, nb, c), lambda j, p: (p, 0, 0, j, 0)),
        out_shape=jax.ShapeDtypeStruct((pairs, 2, h1, n2, c), BF16),
        scratch_shapes=[pltpu.VMEM((2, c // LANES, n1 * nb, LANES), F32),
                        pltpu.VMEM((2, c // LANES, h1 * nb, LANES), F32)],
        compiler_params=_cparams("arbitrary", "arbitrary"),
        name="fft3",
    )(s2, f1i, twr, twi)


def _hyena_long_conv(u, hf, hb, asum):
    b, l, c = u.shape
    n, n1, n2 = _fft_dims(l)
    h1 = n1 // 2
    f1f, f1i, f2f, f2i, twr, twi = _fft_tables(l, 128)
    zeros = jnp.zeros((l, c), BF16)
    taps = jnp.stack([hf.astype(BF16), zeros, hb.astype(BF16), zeros]).reshape(4, h1, n2, c)
    s1k = _fft1(taps, f1f, twr, twi).reshape(2, 2, n1, n2, c)
    khat = _fft_kernel_spectrum(s1k, f2f, asum, n=n)
    s1 = _fft1(u.reshape(b, h1, n2, c), f1f, twr, twi).reshape(b // 2, 2, n1, n2, c)
    s2 = _fft2(s1, khat, f2f, f2i)
    return _fft3(s2, f1i, twr, twi).reshape(b, l, c)


def _ret_kernel(q_ref, k_ref, v_ref, g_ref, cos_ref, sin_ref, pm_ref, lgf_ref, lgb_ref, sf0_ref, sb0_ref,
                o_ref, sf_ref, sb_ref, sbs_ref, *, nb, dk):
    s = pl.program_id(2)
    ch = RET_CHUNK
    nc = k_ref.shape[0] // ch
    lgf = lgf_ref[...]
    lgb = lgb_ref[...]
    ii = lax.broadcasted_iota(jnp.int32, (ch, dk), 0).astype(F32)
    scale = dk ** -0.5

    def rot(x_bf, r0):
        sw = _bdot(x_bf, pm_ref[...])
        return x_bf.astype(F32) * cos_ref[r0:r0 + ch, :] + sw * sin_ref[r0:r0 + ch, :]

    @pl.when(s == 0)
    def _():
        sb_ref[...] = sb0_ref[...]

    @pl.when(s == nb)
    def _():
        sf_ref[...] = sf0_ref[...]

    @pl.when(s < nb)
    def _():
        blk = nb - 1 - s
        zb = jnp.exp(lgb * ii)
        gch = jnp.exp(lgb * float(ch))
        for cc in reversed(range(nc)):
            r0 = cc * ch
            kc = rot(k_ref[r0:r0 + ch, :], r0) * scale
            sbs_ref[blk * nc + cc] = sb_ref[...]
            upd = _bdot((kc * zb).T.astype(BF16), v_ref[r0:r0 + ch, :])
            sb_ref[...] = sb_ref[...] * gch + upd

    @pl.when(s >= nb)
    def _():
        blk = s - nb
        ri = lax.broadcasted_iota(jnp.int32, (ch, ch), 0)
        ci = lax.broadcasted_iota(jnp.int32, (ch, ch), 1)
        rel = (ri - ci).astype(F32)
        dmat = jnp.where(ri > ci, jnp.exp(lgf * jnp.maximum(rel, 0.0)),
                         jnp.where(ri < ci, jnp.exp(lgb * jnp.maximum(-rel, 0.0)), 2.0))
        xf = jnp.exp(lgf * (ii + 1.0))
        xb = jnp.exp(lgb * (float(ch) - ii))
        zf = jnp.exp(lgf * (float(ch) - 1.0 - ii))
        gch = jnp.exp(lgf * float(ch))
        for cc in range(nc):
            r0 = cc * ch
            qc = rot(q_ref[r0:r0 + ch, :], r0)
            kc = rot(k_ref[r0:r0 + ch, :], r0) * scale
            vc = v_ref[r0:r0 + ch, :]
            sc = _bdot_nt(qc.astype(BF16), kc.astype(BF16)) * dmat
            qx = jnp.concatenate([qc * xf, qc * xb], axis=1).astype(BF16)
            st = jnp.concatenate([sf_ref[...], sbs_ref[blk * nc + cc]], axis=0).astype(BF16)
            o = _bdot(sc.astype(BF16), vc) + _bdot(qx, st)
            sf_ref[...] = sf_ref[...] * gch + _bdot((kc * zf).T.astype(BF16), vc)
            y = o * lax.rsqrt(jnp.mean(o * o, axis=-1, keepdims=True) + EPS)
            o_ref[r0:r0 + ch, :] = (y * _silu(g_ref[r0:r0 + ch, :].astype(F32))).astype(BF16)


def _retention(p, cosf, sinf, pm, lgf, lgb, sf0, sb0, *, q_off, k_off, v_off, g_off, heads, dk, dv, tb):
    b, l, _ = p.shape
    nb = l // tb
    fwd = lambda s: jnp.maximum(s - nb, 0)
    both = lambda s: jnp.where(s < nb, nb - 1 - s, s - nb)
    return pl.pallas_call(
        functools.partial(_ret_kernel, nb=nb, dk=dk),
        grid=(b, heads, 2 * nb),
        in_specs=[pl.BlockSpec((None, tb, dk), lambda i, h, s: (i, fwd(s), q_off // dk + h)),
                  pl.BlockSpec((None, tb, dk), lambda i, h, s: (i, both(s), k_off // dk + h)),
                  pl.BlockSpec((None, tb, dv), lambda i, h, s: (i, both(s), v_off // dv + h)),
                  pl.BlockSpec((None, tb, dv), lambda i, h, s: (i, fwd(s), g_off // dv + h)),
                  pl.BlockSpec((tb, dk), lambda i, h, s: (both(s), 0)),
                  pl.BlockSpec((tb, dk), lambda i, h, s: (both(s), 0)),
                  pl.BlockSpec((dk, dk), lambda i, h, s: (0, 0)),
                  pl.BlockSpec((None, 1, 1), lambda i, h, s: (h, 0, 0)),
                  pl.BlockSpec((None, 1, 1), lambda i, h, s: (h, 0, 0)),
                  pl.BlockSpec((None, None, dk, dv), lambda i, h, s: (i, h, 0, 0)),
                  pl.BlockSpec((None, None, dk, dv), lambda i, h, s: (i, h, 0, 0))],
        out_specs=pl.BlockSpec((None, tb, dv), lambda i, h, s: (i, fwd(s), h)),
        out_shape=jax.ShapeDtypeStruct((b, l, heads * dv), BF16),
        scratch_shapes=[pltpu.VMEM((dk, dv), F32), pltpu.VMEM((dk, dv), F32),
                        pltpu.VMEM((l // RET_CHUNK, dk, dv), F32)],
        compiler_params=_cparams("arbitrary", "arbitrary", "arbitrary"),
        name="ret",
    )(p, p, p, p, cosf, sinf, pm, lgf, lgb, sf0, sb0)


def _merge_kernel(y_ref, u_ref, x0_ref, r_ref, ah_ref, ar_ref, x_ref, g1_ref, sh_ref, sc_ref, n2_ref, hb_ref,
                  wh_ref, wr_ref, wo_ref, x2_ref, h2_ref):
    yh = (y_ref[...].astype(F32) + u_ref[...].astype(F32) * hb_ref[...]) * x0_ref[...].astype(F32)
    y_h = _bdot(yh.astype(BF16), wh_ref[...])
    y_r = _bdot(r_ref[...], wr_ref[...])
    m = _sigmoid(ah_ref[...].astype(F32)) * y_h + _sigmoid(ar_ref[...].astype(F32)) * y_r
    x2 = x_ref[...] + g1_ref[...] * _bdot(m.astype(BF16), wo_ref[...])
    x2_ref[...] = x2
    h2_ref[...] = (_rms(x2, n2_ref[...]) * (1.0 + sc_ref[...]) + sh_ref[...]).astype(BF16)


def _merge(y, u, x0, r, p, x, g1, sh2, sc2, norm2, hy_bias, wh, wr, wo, *, ah_off, ar_off, tm):
    b, l, d = x.shape
    c = y.shape[2]
    rw = r.shape[2]
    tile = lambda w, cb=0: pl.BlockSpec((None, tm, w), lambda i, j: (i, j, cb))
    vec = pl.BlockSpec((None, 1, d), lambda i, j: (i, 0, 0))
    return pl.pallas_call(
        _merge_kernel,
        grid=(b, l // tm),
        in_specs=[tile(c), tile(c), tile(c), tile(rw), tile(d, ah_off // d), tile(d, ar_off // d), tile(d),
                  vec, vec, vec,
                  pl.BlockSpec((1, d), lambda i, j: (0, 0)),
                  pl.BlockSpec((1, c), lambda i, j: (0, 0)),
                  _const_spec((c, d)), _const_spec((rw, d)), _const_spec((d, d))],
        out_specs=[tile(d), tile(d)],
        out_shape=[jax.ShapeDtypeStruct((b, l, d), F32), jax.ShapeDtypeStruct((b, l, d), BF16)],
        compiler_params=_cparams("arbitrary", "arbitrary"),
        name="merge",
    )(y, u, x0, r, p, p, x, g1, sh2, sc2, norm2, hy_bias, wh, wr, wo)


LOWEST_BITS = -0x00800001


def _rank_code(r):
    return np.array(LOWEST_BITS - r, np.int32).view(np.float32).item()


def _top_sorted(s, k):
    vals = []
    for r in range(k):
        m = jnp.max(s, axis=0, keepdims=True)
        vals.append(m)
        s = jnp.where(s == m, _rank_code(r), s)
    coded = s <= _rank_code(k - 1)
    rank = jnp.where(coded, LOWEST_BITS - pltpu.bitcast(s, jnp.int32), k).astype(F32)
    return vals, rank


def _pair_pack(v):
    b = pltpu.bitcast(v.astype(BF16).astype(F32), jnp.uint32)
    return b | (b >> 16)


def _row_bcast(row, rows):
    rep = pltpu.bitcast(jnp.broadcast_to(row, (8, row.shape[1])), BF16)
    return jnp.tile(rep, (rows // 16, 1))


def _peer_pairs(k):
    return [(i, j) for i in range(k) for j in range(k) if (i + 1) * (j + 1) <= k]


def _peer_kernel(h_ref, x_ref, g2_ref, fn_ref, wq_ref, kd_ref, u_ref, vt_ref, o_ref,
                 cn_ref, e1_ref, rk_ref, e2_ref, q_ref, s_ref, cand_ref, ht_ref, a_ref, w_ref, acc_ref,
                 *, topk, split):
    j = pl.program_id(2)
    nk = PEER_N_KEYS
    t = h_ref.shape[0]
    eb = u_ref.shape[0]
    pairs = _peer_pairs(topk)

    @pl.when(j == 0)
    def _():
        q = _bdot(h_ref[...], wq_ref[...]).astype(BF16)
        for hd in range(PEER_HEADS):
            q_ref[hd] = q[:, hd * PEER_DK:(hd + 1) * PEER_DK]
        cand_ref[...] = jnp.full(cand_ref.shape, NEG_INF, F32)

        def head(hd, s_ref, cand_ref):
            s_ref[...] = _bdot_nt(kd_ref[hd], q_ref[hd])
            for c0 in range(0, t, LANES):
                cols = slice(c0, c0 + LANES)
                s1, s2 = s_ref[0:nk, cols], s_ref[nk:, cols]
                a, rank1 = _top_sorted(s1, topk)
                b, rank2 = _top_sorted(s2, topk)
                for r, (pi, pj) in enumerate(pairs):
                    cand_ref[c0 // LANES, r:r + 1, :] = a[pi] + b[pj]
                cs = cand_ref[c0 // LANES]
                tau = _top_sorted(cs, topk)[0][-1]
                sel = cs >= tau
                mx = a[0] + b[0]
                z = jnp.sum(jnp.where(sel, jnp.exp(cs - mx), 0.0), axis=0, keepdims=True)
                self = jnp.where(sel, 1.0, 0.0)
                cnt1 = jnp.zeros(s1.shape, F32)
                for c in range(topk // 2):
                    height = sum(self[r:r + 1] for r, (_, pj) in enumerate(pairs) if pj == c)
                    cnt1 = jnp.where(rank1 < height, float(c + 1), cnt1)
                width0 = sum(self[r:r + 1] for r, (pi, _) in enumerate(pairs) if pi == 0)
                cnt1 = jnp.where(rank1 < 1.0, width0, cnt1)
                cn_ref[hd, :, cols] = _pair_pack(cnt1)
                e1_ref[hd, :, cols] = _pair_pack(jnp.exp(s1 - a[0]))
                rk_ref[hd, :, cols] = rank2.astype(BF16)
                e2_ref[hd, :, cols] = (jnp.exp(s2 - b[0]) / z).astype(BF16)

        def head_group(i, carry):
            for k in range(HEAD_UNROLL):
                head(i * HEAD_UNROLL + k, s_ref.at[k], cand_ref.at[k])
            return carry

        lax.fori_loop(0, PEER_HEADS // HEAD_UNROLL, head_group, 0)
        acc_ref[...] = jnp.zeros_like(acc_ref)
        ht_ref[...] = h_ref[...].astype(F32).T.astype(BF16)

    part = eb // split
    zero = jnp.zeros((), BF16)

    for s in range(split):
        rows = slice(s * part, (s + 1) * part)
        a_ref[rows, :] = _bdot(u_ref[rows, :], ht_ref[...]).astype(BF16)
    for s in range(split):
        rows = slice(s * part, (s + 1) * part)
        for g in range(part // nk):
            r0 = s * part + g * nk
            i = j * (eb // nk) + r0 // nk
            act = _gelu_tanh(a_ref[r0:r0 + nk, :])
            gate = jnp.zeros((nk, t), BF16)
            for hd in range(PEER_HEADS):
                cnt = _row_bcast(cn_ref[hd, pl.ds(i, 1), :], nk)
                e1 = _row_bcast(e1_ref[hd, pl.ds(i, 1), :], nk)
                gate = gate + jnp.where(rk_ref[hd] < cnt, e2_ref[hd], zero) * e1
            w_ref[r0:r0 + nk, :] = gate * act
        acc_ref[...] += _bdot(vt_ref[:, rows], w_ref[rows, :])

    @pl.when(j == pl.num_programs(2) - 1)
    def _():
        x3 = x_ref[...] + g2_ref[...] * acc_ref[...].T
        o_ref[...] = _rms(x3, fn_ref[...])


def _peer(h2, x2, g2, fnorm, wq, kd, u_bf, vt_bf, *, t, eb):
    b, l, d = x2.shape
    e = u_bf.shape[0]
    nk = PEER_N_KEYS
    npair = -(-len(_peer_pairs(PEER_TOPK)) // 8) * 8
    nj = e // eb
    tile =pl.BlockSpec((None, t, d), lambda i, m, j: (i, m, 0))
    return pl.pallas_call(
        functools.partial(_peer_kernel, topk=PEER_TOPK, split=4),
        grid=(b, l // t, nj),
        in_specs=[tile, tile,
                  pl.BlockSpec((None, 1, d), lambda i, m, j: (i, 0, 0)),
                  pl.BlockSpec((1, d), lambda i, m, j: (0, 0)),
                  _const_spec((d, PEER_HEADS * PEER_DK)),
                  _const_spec((PEER_HEADS, 2 * nk, PEER_DK)),
                  pl.BlockSpec((eb, d), lambda i, m, j: (j, 0)),
                  pl.BlockSpec((d, eb), lambda i, m, j: (0, j))],
        out_specs=tile,
        out_shape=jax.ShapeDtypeStruct((b, l, d), F32),
        scratch_shapes=[pltpu.VMEM((PEER_HEADS, nk, t), jnp.uint32), pltpu.VMEM((PEER_HEADS, nk, t), jnp.uint32),
                        pltpu.VMEM((PEER_HEADS, nk, t), BF16), pltpu.VMEM((PEER_HEADS, nk, t), BF16),
                        pltpu.VMEM((PEER_HEADS, t, PEER_DK), BF16),
                          pltpu.VMEM((HEAD_UNROLL, 2 * nk, t), F32),
                          pltpu.VMEM((HEAD_UNROLL, t // LANES, npair, LANES), F32),
                          pltpu.VMEM((d, t), BF16),
                          pltpu.VMEM((eb, t), BF16), pltpu.VMEM((eb, t), BF16),
                          pltpu.VMEM((d, t), F32)],
        compiler_params=_cparams("arbitrary", "arbitrary", "arbitrary"),
        name="peer",
    )(h2, x2, g2, fnorm, wq, kd, u_bf, vt_bf)


def _filter_features(l, emb):
    bands_n = (emb - 1) // 2
    t = jnp.linspace(0.0, 1.0, l, dtype=F32)[:, None]
    bands = jnp.linspace(1e-4, bands_n - 1, bands_n, dtype=F32)[None, :]
    w = (2.0 * math.pi / l) * jnp.arange(l, dtype=F32)[:, None]
    z = jnp.concatenate([t, jnp.cos(bands * w), -jnp.sin(bands * w)], axis=-1)
    return t, z


def _rotary_tables(l, dk):
    half = dk // 2
    nf = half // 2
    inv = ROPE_BASE ** (-jnp.arange(nf, dtype=F32) / nf)
    pos = jnp.arange(l, dtype=jnp.int32)
    rows = (pos // GRID_W).astype(F32)
    cols = (pos % GRID_W).astype(F32)
    ar = rows[:, None] * inv[None, :]
    ac = cols[:, None] * inv[None, :]
    cosf = jnp.concatenate([jnp.cos(ar), jnp.cos(ar), jnp.cos(ac), jnp.cos(ac)], axis=1)
    sinf = jnp.concatenate([-jnp.sin(ar), jnp.sin(ar), -jnp.sin(ac), jnp.sin(ac)], axis=1)
    lane = jnp.arange(dk, dtype=jnp.int32)
    pm = (lane[:, None] == (lane[None, :] ^ nf)).astype(BF16)
    return cosf, sinf, pm


def kernel(x, c, ctx, c_ctx, w_ada, b_ada, norm1, norm2, w_in, hy_conv_w, hy_conv_b, hy_fw1, hy_fb1, hy_fw2, hy_fb2, hy_fw3, hy_fb3, hy_fw4, hy_sin_freq, hy_deltas, hy_bias, ret_log_decay_f, ret_log_decay_b, w_hy_out, w_ret_out, w_o, peer_w_query, peer_sub_keys, peer_u, peer_v, final_norm):
    assert w_ada.shape[0] == 1, "single-layer configuration"
    b, l, d = x.shape
    assert b % 2 == 0 and l % RET_CHUNK == 0
    heads = RET_HEADS
    dk = d // 8
    dv = 2 * dk
    c_hy = d
    hy_cols = 3 * c_hy
    q_off = hy_cols
    k_off = q_off + heads * dk
    v_off = k_off + heads * dk
    g_off = v_off + heads * dv
    ah_off = g_off + heads * dv
    ar_off = ah_off + d

    rows = -(-(b + 1) // 8) * 8
    cc = jnp.zeros((rows, d), F32).at[:b].set(c).at[b].set(c_ctx)
    mod = _mod(cc, w_ada[0], b_ada[0][None, :])
    mod_l = mod[:b].reshape(b, 6, 1, d)
    sh1, sc1, g1, sh2, sc2, g2 = (mod_l[:, i] for i in range(6))
    mod_c = mod[b].reshape(6, 1, d)
    csh1, csc1 = mod_c[0], mod_c[1]

    w_in_bf = w_in[0].astype(BF16)
    n1g = norm1[0][None, :]
    lgf = ret_log_decay_f[0]
    lgb = ret_log_decay_b[0]

    sf0, sb0 = _ctx_states(ctx, csh1, csc1, n1g, w_in_bf, lgf[:, None], lgb[:, None],
                           k_off=k_off, v_off=v_off, heads=heads, dk=dk, dv=dv)

    p = _inproj(x, sh1, sc1, n1g, w_in_bf, tm=min(512, l))

    u, x0c = _hy_pre(p, hy_conv_w[0], hy_conv_b[0][None, :], c=c_hy, tl=min(1024, l))
    emb = hy_fw1.shape[1]
    order = hy_fw1.shape[2]
    t_lin, z = _filter_features(l, emb)
    epad = -(-emb // 64) * 64
    z = jnp.pad(z, ((0, 0), (0, epad - emb)))
    fw1 = jnp.pad(hy_fw1[0], ((0, epad - emb), (0, 0)))
    hf, hb, asum = _hy_filter(z, t_lin, fw1, hy_fb1[0][None, :], hy_fw2[0], hy_fb2[0][None, :],
                              hy_fw3[0], hy_fb3[0][None, :], hy_fw4[0], hy_sin_freq[0][None, :],
                              hy_deltas[0][None, :], tl=min(1024, l))
    del order
    y = _hyena_long_conv(u, hf, hb, asum)

    cosf, sinf, pm = _rotary_tables(l, dk)
    r = _retention(p, cosf, sinf, pm, lgf[:, None, None], lgb[:, None, None], sf0, sb0,
                   q_off=q_off, k_off=k_off, v_off=v_off, g_off=g_off, heads=heads, dk=dk, dv=dv,
                   tb=min(4096, l))

    x2, h2 = _merge(y, u, x0c, r, p, x, g1, sh2, sc2, norm2[0][None, :], hy_bias[0][None, :],
                    w_hy_out[0].astype(BF16), w_ret_out[0].astype(BF16), w_o[0].astype(BF16),
                    ah_off=ah_off, ar_off=ar_off, tm=min(512, l))

    sk = peer_sub_keys[0]
    hk = sk.shape[3]
    zk = jnp.zeros_like(sk[:, 0])
    kd = jnp.concatenate([jnp.concatenate([sk[:, 0], zk], axis=2),
                          jnp.concatenate([zk, sk[:, 1]], axis=2)], axis=1).astype(BF16)
    del hk
    out = _peer(h2, x2, g2, final_norm[None, :], peer_w_query[0].astype(BF16), kd,
                peer_u[0].astype(BF16), peer_v[0].T.astype(BF16), t=min(512, l), eb=2048)
    return out
```

```python
import functools
import math

import jax
import jax.numpy as jnp
import numpy as np
from jax import lax
from jax.experimental import pallas as pl
from jax.experimental.pallas import tpu as pltpu

F32 = jnp.float32
BF16 = jnp.bfloat16
EPS = 1e-6

GRID_W = 64
RET_HEADS = 4
RET_CHUNK = 256
ROPE_BASE = 10000.0
PEER_HEADS = 8
PEER_N_KEYS = 128
PEER_TOPK = 16
PEER_DK = 128
HEAD_UNROLL = 4

VMEM_LIMIT_BYTES = 56 * 1024 * 1024
LANES = 128
SUBLANES = 8
NEG_INF = float("-inf")


def _cparams(*sem):
    return pltpu.CompilerParams(dimension_semantics=sem, vmem_limit_bytes=VMEM_LIMIT_BYTES)


def _bdot(a, b):
    return jnp.dot(a, b, preferred_element_type=F32)


def _bdot_nt(a, b):
    return lax.dot_general(a, b, (((1,), (1,)), ((), ())), preferred_element_type=F32)


def _split(a):
    hi = a.astype(BF16)
    lo = (a - hi.astype(F32)).astype(BF16)
    return hi, lo


def _dot3(a, b):
    ah, al = _split(a)
    bh, bl = _split(b)
    return _bdot(ah, bh) + _bdot(ah, bl) + _bdot(al, bh)


def _sigmoid(x):
    return 1.0 / (1.0 + jnp.exp(-x))


def _silu(x):
    return x * _sigmoid(x)


def _gelu_tanh(x):
    k = 2.0 * math.sqrt(2.0 / math.pi) * math.log2(math.e)
    return x / (1.0 + jnp.exp2(-(x * (k + (k * 0.044715) * (x * x)))))


def _rms(x, gain):
    return x * lax.rsqrt(jnp.mean(x * x, axis=-1, keepdims=True) + EPS) * gain


def _const_spec(shape):
    nd = len(shape)
    return pl.BlockSpec(shape, lambda *_: (0,) * nd, pipeline_mode=pl.Buffered(1))


def _mod_kernel(c_ref, w_ref, b_ref, o_ref):
    o_ref[...] = _dot3(_silu(c_ref[...]), w_ref[...]) + b_ref[...]


def _mod(cc, w_ada, b_ada):
    rows, d = cc.shape
    n = w_ada.shape[1]
    return pl.pallas_call(
        _mod_kernel,
        grid=(n // d,),
        in_specs=[pl.BlockSpec((rows, d), lambda j: (0, 0)),
                  pl.BlockSpec((d, d), lambda j: (0, j)),
                  pl.BlockSpec((1, d), lambda j: (0, j))],
        out_specs=pl.BlockSpec((rows, d), lambda j: (0, j)),
        out_shape=jax.ShapeDtypeStruct((rows, n), F32),
        compiler_params=_cparams("arbitrary"),
        name="mod",
    )(cc, w_ada, b_ada)


def _ctx_kernel(ctx_ref, sh_ref, sc_ref, g_ref, wk_ref, wv_ref, lgf_ref, lgb_ref, sf_ref, sb_ref, *, heads, dk, dv):
    n = ctx_ref.shape[0]
    h = (_rms(ctx_ref[...], g_ref[...]) * (1.0 + sc_ref[...]) + sh_ref[...]).astype(BF16)
    kc = _bdot(h, wk_ref[...]) * (dk ** -0.5)
    vc = _bdot(h, wv_ref[...]).astype(BF16)
    pos = lax.broadcasted_iota(jnp.int32, (n, dk), 0).astype(F32)
    for hd in range(heads):
        kh = kc[:, hd * dk:(hd + 1) * dk]
        vh = vc[:, hd * dv:(hd + 1) * dv]
        wf = jnp.exp(lgf_ref[hd:hd + 1, :] * (n - 1.0 - pos))
        wb = jnp.exp(lgb_ref[hd:hd + 1, :] * pos)
        sf_ref[hd] = _bdot((kh * wf).T.astype(BF16), vh)
        sb_ref[hd] = _bdot((kh * wb).T.astype(BF16), vh)


def _ctx_states(ctx, csh, csc, gain, w_in_bf, lgf, lgb, *, k_off, v_off, heads, dk, dv):
    b, n, d = ctx.shape
    kw, vw = heads * dk, heads * dv
    out = jax.ShapeDtypeStruct((b, heads, dk, dv), F32)
    return pl.pallas_call(
        functools.partial(_ctx_kernel, heads=heads, dk=dk, dv=dv),
        grid=(b,),
        in_specs=[pl.BlockSpec((None, n, d), lambda i: (i, 0, 0)),
                  pl.BlockSpec((1, d), lambda i: (0, 0)),
                  pl.BlockSpec((1, d), lambda i: (0, 0)),
                  pl.BlockSpec((1, d), lambda i: (0, 0)),
                  pl.BlockSpec((d, kw), lambda i: (0, k_off // kw)),
                  pl.BlockSpec((d, vw), lambda i: (0, v_off // vw)),
                  pl.BlockSpec((heads, 1), lambda i: (0, 0)),
                  pl.BlockSpec((heads, 1), lambda i: (0, 0))],
        out_specs=[pl.BlockSpec((None, heads, dk, dv), lambda i: (i, 0, 0, 0)),
                   pl.BlockSpec((None, heads, dk, dv), lambda i: (i, 0, 0, 0))],
        out_shape=[out, out],
        compiler_params=_cparams("arbitrary"),
        name="ctx",
    )(ctx, csh, csc, gain, w_in_bf, w_in_bf, lgf, lgb)


def _inproj_kernel(x_ref, sh_ref, sc_ref, g_ref, w_ref, o_ref, *, nc):
    h = (_rms(x_ref[...], g_ref[...]) * (1.0 + sc_ref[...]) + sh_ref[...]).astype(BF16)
    n = w_ref.shape[1]
    for j in range(n // nc):
        o_ref[:, j * nc:(j + 1) * nc] = _bdot(h, w_ref[:, j * nc:(j + 1) * nc]).astype(BF16)


def _inproj(x, sh, sc, gain, w_bf, *, tm):
    b, l, d = x.shape
    n = w_bf.shape[1]
    return pl.pallas_call(
        functools.partial(_inproj_kernel, nc=1024),
        grid=(b, l // tm),
        in_specs=[pl.BlockSpec((None, tm, d), lambda i, j: (i, j, 0)),
                  pl.BlockSpec((None, 1, d), lambda i, j: (i, 0, 0)),
                  pl.BlockSpec((None, 1, d), lambda i, j: (i, 0, 0)),
                  pl.BlockSpec((1, d), lambda i, j: (0, 0)),
                  _const_spec((d, n))],
        out_specs=pl.BlockSpec((None, tm, n), lambda i, j: (i, j, 0)),
        out_shape=jax.ShapeDtypeStruct((b, l, n), BF16),
        compiler_params=_cparams("arbitrary", "arbitrary"),
        name="inproj",
    )(x, sh, sc, gain, w_bf)


HALO = 16


def _hy_pre_kernel(p_ref, pp_ref, pn_ref, w_ref, b_ref, u_ref, x0_ref, *, c, cw):
    i = pl.program_id(1)
    first = i == 0
    last = i == pl.num_programs(1) - 1
    tl = p_ref.shape[0]
    row = lax.broadcasted_iota(jnp.int32, (tl, cw), 0)
    for cb in range(c // cw):
        zs = []
        for part in range(3):
            c0 = part * c + cb * cw
            cur = p_ref[:, c0:c0 + cw].astype(F32)
            prev = pp_ref[:, c0:c0 + cw].astype(F32)[HALO - 1:HALO, :]
            nxt = pn_ref[:, c0:c0 + cw].astype(F32)[0:1, :]
            prev = jnp.where(first, 0.0, prev)
            nxt = jnp.where(last, 0.0, nxt)
            up = jnp.where(row == 0, prev, pltpu.roll(cur, 1, 0))
            dn = jnp.where(row == tl - 1, nxt, pltpu.roll(cur, tl - 1, 0))
            w = w_ref[:, c0:c0 + cw]
            zs.append(up * w[0:1, :] + cur * w[1:2, :] + dn * w[2:3, :] + b_ref[:, c0:c0 + cw])
        x0, x1, v = zs
        u_ref[:, cb * cw:(cb + 1) * cw] = (v * x1).astype(BF16)
        x0_ref[:, cb * cw:(cb + 1) * cw] = x0.astype(BF16)


def _hy_pre(p, conv_w, conv_b, *, c, tl):
    b, l, _ = p.shape
    hb = tl // HALO
    nh = l // HALO
    out = jax.ShapeDtypeStruct((b, l, c), BF16)
    return pl.pallas_call(
        functools.partial(_hy_pre_kernel, c=c, cw=256),
        grid=(b, l // tl),
        in_specs=[pl.BlockSpec((None, tl, 3 * c), lambda i, j: (i, j, 0)),
                  pl.BlockSpec((None, HALO, 3 * c), lambda i, j: (i, jnp.maximum(j * hb - 1, 0), 0)),
                  pl.BlockSpec((None, HALO, 3 * c), lambda i, j: (i, jnp.minimum((j + 1) * hb, nh - 1), 0)),
                  pl.BlockSpec((3, 3 * c), lambda i, j: (0, 0)),
                  pl.BlockSpec((1, 3 * c), lambda i, j: (0, 0))],
        out_specs=[pl.BlockSpec((None, tl, c), lambda i, j: (i, j, 0)),
                   pl.BlockSpec((None, tl, c), lambda i, j: (i, j, 0))],
        out_shape=[out, out],
        compiler_params=_cparams("arbitrary", "arbitrary"),
        name="hy_pre",
    )(p, p, p, conv_w, conv_b)


def _filt_kernel(z_ref, t_ref, w1, b1, w2, b2, w3, b3, w4, fr, dl, hf_ref, hb_ref, as_ref, *, c):
    i = pl.program_id(0)
    tl = z_ref.shape[0]
    f = fr[...]
    h = jnp.sin(f * (_dot3(z_ref[...], w1[...]) + b1[...]))
    h = jnp.sin(f * (_dot3(h, w2[...]) + b2[...]))
    h = jnp.sin(f * (_dot3(h, w3[...]) + b3[...]))
    h4 = _dot3(h, w4[...])
    win = jnp.exp(-t_ref[...] * jnp.abs(dl[...]))
    hf = h4[:, :c] * win
    row = lax.broadcasted_iota(jnp.int32, (tl, c), 0) + i * tl
    hb = jnp.where(row == 0, 0.0, h4[:, c:] * win)

    hf_ref[...] = hf
    hb_ref[...] = hb

    @pl.when(i == 0)
    def _():
        as_ref[...] = jnp.zeros_like(as_ref)

    as_ref[...] += jnp.sum(jnp.abs(hf) + jnp.abs(hb), axis=0, keepdims=True)


def _hy_filter(z, t, w1, b1, w2, b2, w3, b3, w4, fr, dl, *, tl):
    l, e = z.shape
    o = w2.shape[0]
    c = dl.shape[1]
    full = lambda shape: pl.BlockSpec(shape, lambda i: (0, 0))
    return pl.pallas_call(
        functools.partial(_filt_kernel, c=c),
        grid=(l // tl,),
        in_specs=[pl.BlockSpec((tl, e), lambda i: (i, 0)),
                  pl.BlockSpec((tl, 1), lambda i: (i, 0)),
                  full((e, o)), full((1, o)), full((o, o)), full((1, o)), full((o, o)), full((1, o)),
                  full((o, 2 * c)), full((1, o)), full((1, c))],
        out_specs=[pl.BlockSpec((tl, c), lambda i: (i, 0)),
                   pl.BlockSpec((tl, c), lambda i: (i, 0)),
                   pl.BlockSpec((1, c), lambda i: (0, 0))],
        out_shape=[jax.ShapeDtypeStruct((l, c), F32), jax.ShapeDtypeStruct((l, c), F32),
                   jax.ShapeDtypeStruct((1, c), F32)],
        compiler_params=_cparams("arbitrary"),
        name="filt",
    )(z, t, w1, b1, w2, b2, w3, b3, w4, fr, dl)


N2_BLOCK = 16
K1_BLOCK = 8


def _fft_dims(l):
    n = 2 * l
    n1 = math.isqrt(n)
    assert n1 * n1 == n and n1 % 32 == 0, "sequence length must make 2L a square of a multiple of 32"
    return n, n1, n1


def _fft_tables(l, lanes):
    n, n1, n2 = _fft_dims(l)
    h1 = n1 // 2
    k1 = jnp.arange(n1, dtype=jnp.int32)
    th1 = ((k1[:, None] * jnp.arange(h1, dtype=jnp.int32)[None, :]) % n1).astype(F32) * (2.0 * math.pi / n1)
    c1, s1 = jnp.cos(th1), jnp.sin(th1)
    f1f = jnp.concatenate([jnp.concatenate([c1, s1], axis=1), jnp.concatenate([-s1, c1], axis=1)], axis=0)
    f1i = f1f.T
    k2 = jnp.arange(n2, dtype=jnp.int32)
    th2 = ((k2[:, None] * k2[None, :]) % n2).astype(F32) * (2.0 * math.pi / n2)
    c2, s2 = jnp.cos(th2), jnp.sin(th2)
    f2f = jnp.concatenate([jnp.concatenate([c2, s2], axis=1), jnp.concatenate([-s2, c2], axis=1)], axis=0)
    f2i = jnp.concatenate([jnp.concatenate([c2, -s2], axis=1), jnp.concatenate([s2, c2], axis=1)], axis=0)
    tht = (k2[:, None] * k1[None, :]).astype(F32) * (2.0 * math.pi / n)
    twr = jnp.broadcast_to(jnp.cos(tht)[:, :, None], (n2, n1, lanes))
    twi = jnp.broadcast_to(jnp.sin(tht)[:, :, None], (n2, n1, lanes))
    return f1f.astype(BF16), f1i.astype(BF16), f2f.astype(BF16), f2i.astype(BF16), twr, twi


def _fft1_kernel(*refs, c, parts):
    x_refs, (f_ref, twr_ref, twi_ref, o_ref, xs_ref) = refs[:parts], refs[parts:]
    h1, nb = x_refs[0].shape[0], x_refs[0].shape[1]
    n1 = 2 * h1
    reps = c // twr_ref.shape[2]
    for ri, x_ref in enumerate(x_refs):
        xf = x_ref[...].reshape(h1 * nb, c).astype(F32).reshape(h1, nb // SUBLANES, SUBLANES, c)
        for g in range(nb // SUBLANES):
            xg = xf[:, g].reshape(h1 * SUBLANES, c)
            for cb in range(c // LANES):
                xs_ref[ri, g, cb] = xg[:, cb * LANES:(cb + 1) * LANES]
    f = f_ref[:, 0:parts * h1]
    for j in range(nb):
        x = jnp.concatenate(
            [jnp.concatenate([xs_ref[ri, j // SUBLANES, cb, pl.ds(j % SUBLANES, h1, stride=SUBLANES), :]
                              for cb in range(c // LANES)], axis=1)
             for ri in range(parts)], axis=0)
        a = _bdot(f, x.astype(BF16))
        ar, ai = a[0:n1], a[n1:]
        wr = jnp.tile(twr_ref[j], (1, reps))
        wi = jnp.tile(twi_ref[j], (1, reps))
        o_ref[0:n1, j * c:(j + 1) * c] = (ar * wr + ai * wi).astype(BF16)
        o_ref[n1:, j * c:(j + 1) * c] = (ai * wr - ar * wi).astype(BF16)


def _fft1(u4, f1f, twr, twi, *, parts):
    b, h1, n2, c = u4.shape
    n1 = 2 * h1
    nb = min(N2_BLOCK, n2)
    lanes = twr.shape[2]
    x_spec = lambda q: pl.BlockSpec((None, h1, nb, c), lambda j, p: (parts * p + q, 0, j, 0))
    return pl.pallas_call(
        functools.partial(_fft1_kernel, c=c, parts=parts),
        grid=(n2 // nb, b // parts),
        in_specs=[x_spec(q) for q in range(parts)]
                 + [pl.BlockSpec((2 * n1, n1), lambda j, p: (0, 0)),
                    pl.BlockSpec((nb, n1, lanes), lambda j, p: (j, 0, 0)),
                    pl.BlockSpec((nb, n1, lanes), lambda j, p: (j, 0, 0))],
        out_specs=pl.BlockSpec((None, 2 * n1, nb * c), lambda j, p: (p, 0, j)),
        out_shape=jax.ShapeDtypeStruct((b // parts, 2 * n1, n2 * c), BF16),
        scratch_shapes=[pltpu.VMEM((parts, nb // SUBLANES, c // LANES, h1 * SUBLANES, LANES), F32)],
        compiler_params=_cparams("arbitrary", "arbitrary"),
        name="fft1",
    )(*([u4] * parts), f1f, twr, twi)


def _fftk_kernel(pr_ref, pi_ref, qr_ref, qi_ref, f_ref, as_ref, o_ref, *, n):
    n2 = pr_ref.shape[1]
    scale = 1.0 / (n * (as_ref[...] + EPS))
    for k in range(pr_ref.shape[0]):
        p = _bdot(f_ref[...], jnp.concatenate([pr_ref[k], pi_ref[k]], axis=0))
        q = _bdot(f_ref[...], jnp.concatenate([qr_ref[k], qi_ref[k]], axis=0))
        o_ref[k, 0:n2, :] = (p[0:n2] + q[0:n2]) * scale
        o_ref[k, n2:, :] = (p[n2:] - q[n2:]) * scale


def _fft_kernel_spectrum(s1, f2f, asum, *, n):
    _, _, n1, n2, c = s1.shape
    kb = K1_BLOCK
    spec = lambda pair, ri: pl.BlockSpec((None, None, kb, n2, c), lambda k: (pair, ri, k, 0, 0))
    return pl.pallas_call(
        functools.partial(_fftk_kernel, n=n),
        grid=(n1 // kb,),
        in_specs=[spec(0, 0), spec(0, 1), spec(1, 0), spec(1, 1),
                  pl.BlockSpec((2 * n2, 2 * n2), lambda k: (0, 0)),
                  pl.BlockSpec((1, c), lambda k: (0, 0))],
        out_specs=pl.BlockSpec((kb, 2 * n2, c), lambda k: (k, 0, 0)),
        out_shape=jax.ShapeDtypeStruct((n1, 2 * n2, c), F32),
        compiler_params=_cparams("arbitrary"),
        name="fftk",
    )(s1, s1, s1, s1, f2f, asum)


def _fft2_kernel(ar_ref, ai_ref, kh_ref, ff_ref, fi_ref, o_ref):
    n2 = ar_ref.shape[1]
    for k in range(ar_ref.shape[0]):
        x = _bdot(ff_ref[...], jnp.concatenate([ar_ref[k], ai_ref[k]], axis=0))
        xr, xi = x[0:n2], x[n2:]
        kr, ki = kh_ref[k, 0:n2, :], kh_ref[k, n2:, :]
        y = jnp.concatenate([xr * kr - xi * ki, xr * ki + xi * kr], axis=0).astype(BF16)
        b = _bdot(fi_ref[...], y)
        o_ref[0, k] = b[0:n2].astype(BF16)
        o_ref[1, k] = b[n2:].astype(BF16)


def _fft2(s1, khat, f2f, f2i):
    pairs, _, n1, n2, c = s1.shape
    kb = K1_BLOCK
    spec = lambda ri: pl.BlockSpec((None, None, kb, n2, c), lambda k, p: (p, ri, k, 0, 0))
    return pl.pallas_call(
        _fft2_kernel,
        grid=(n1 // kb, pairs),
        in_specs=[spec(0), spec(1),
                  pl.BlockSpec((kb, 2 * n2, c), lambda k, p: (k, 0, 0)),
                  pl.BlockSpec((2 * n2, 2 * n2), lambda k, p: (0, 0)),
                  pl.BlockSpec((2 * n2, 2 * n2), lambda k, p: (0, 0))],
        out_specs=pl.BlockSpec((None, 2, kb, n2, c), lambda k, p: (p, 0, k, 0, 0)),
        out_shape=jax.ShapeDtypeStruct((pairs, 2, n1, n2, c), BF16),
        compiler_params=_cparams("arbitrary", "arbitrary"),
        name="fft2",
    )(s1, s1, khat, f2f, f2i)


def _fft3_kernel(b_ref, f_ref, twr_ref, twi_ref, o_ref, bs_ref, ys_ref, *, c):
    n1 = twr_ref.shape[1]
    h1 = n1 // 2
    nb = twr_ref.shape[0]
    reps = c // twr_ref.shape[2]
    for ri in range(2):
        bf = b_ref[ri].reshape(n1 * nb, c).astype(F32).reshape(n1, nb // SUBLANES, SUBLANES, c)
        for g in range(nb // SUBLANES):
            bg = bf[:, g].reshape(n1 * SUBLANES, c)
            for cb in range(c // LANES):
                bs_ref[ri, g, cb] = bg[:, cb * LANES:(cb + 1) * LANES]
    for j in range(nb):
        g, jj = divmod(j, SUBLANES)
        br, bi = (jnp.concatenate([bs_ref[ri, g, cb, pl.ds(jj, n1, stride=SUBLANES), :]
                                   for cb in range(c // LANES)], axis=1)
                  for ri in range(2))
        wr = jnp.tile(twr_ref[j], (1, reps))
        wi = jnp.tile(twi_ref[j], (1, reps))
        t = jnp.concatenate([br * wr - bi * wi, bi * wr + br * wi], axis=0).astype(BF16)
        y = _bdot(f_ref[...], t)
        for ri in range(2):
            for cb in range(c // LANES):
                ys_ref[ri, g, cb, pl.ds(jj, h1, stride=SUBLANES), :] = (
                    y[ri * h1:(ri + 1) * h1, cb * LANES:(cb + 1) * LANES])
    for ri in range(2):
        yg = [jnp.concatenate([ys_ref[ri, g, cb] for cb in range(c // LANES)], axis=1).reshape(h1, SUBLANES, c)
              for g in range(nb // SUBLANES)]
        o_ref[ri] = jnp.concatenate(yg, axis=1).astype(BF16)


def _fft3(s2, f1i, twr, twi):
    pairs, _, n1, n2, c = s2.shape
    h1 = n1 // 2
    nb = min(N2_BLOCK, n2)
    lanes = twr.shape[2]
    return pl.pallas_call(
        functools.partial(_fft3_kernel, c=c),
        grid=(n2 // nb, pairs),
        in_specs=[pl.BlockSpec((None, 2, n1, nb, c), lambda j, p: (p, 0, 0, j, 0)),
                  pl.BlockSpec((n1, 2 * n1), lambda j, p: (0, 0)),
                  pl.BlockSpec((nb, n1, lanes), lambda j, p: (j, 0, 0)),
                  pl.BlockSpec((nb, n1, lanes), lambda j, p: (j, 0, 0))],
        out_specs=pl.BlockSpec((None, 2, h1, nb, c), lambda j, p: (p, 0, 0, j, 0)),
        out_shape=jax.ShapeDtypeStruct((pairs, 2, h1, n2, c), BF16),
        scratch_shapes=[pltpu.VMEM((2, nb // SUBLANES, c // LANES, n1 * SUBLANES, LANES), F32),
                        pltpu.VMEM((2, nb // SUBLANES, c // LANES, h1 * SUBLANES, LANES), F32)],
        compiler_params=_cparams("arbitrary", "arbitrary"),
        name="fft3",
    )(s2, f1i, twr, twi)


def _hyena_long_conv(u, hf, hb, asum):
    b, l, c = u.shape
    n, n1, n2 = _fft_dims(l)
    h1 = n1 // 2
    f1f, f1i, f2f, f2i, twr, twi = _fft_tables(l, 128)
    taps = jnp.stack([hf, hb]).astype(BF16).reshape(2, h1, n2, c)
    s1k = _fft1(taps, f1f, twr, twi, parts=1).reshape(2, 2, n1, n2, c)
    khat = _fft_kernel_spectrum(s1k, f2f, asum, n=n)
    s1 = _fft1(u.reshape(b, h1, n2, c), f1f, twr, twi, parts=2).reshape(b // 2, 2, n1, n2, c)
    s2 = _fft2(s1, khat, f2f, f2i)
    return _fft3(s2, f1i, twr, twi).reshape(b, l, c)


def _ret_kernel(q_ref, k_ref, v_ref, g_ref, cos_ref, sin_ref, pm_ref, lgf_ref, lgb_ref, sf0_ref, sb0_ref,
                o_ref, sf_ref, sb_ref, sbs_ref, *, nb, dk):
    s = pl.program_id(2)
    ch = RET_CHUNK
    nc = k_ref.shape[0] // ch
    lgf = lgf_ref[...]
    lgb = lgb_ref[...]
    ii = lax.broadcasted_iota(jnp.int32, (ch, dk), 0).astype(F32)
    scale = dk ** -0.5

    def rot(x_bf, r0):
        sw = _bdot(x_bf, pm_ref[...])
        return x_bf.astype(F32) * cos_ref[r0:r0 + ch, :] + sw * sin_ref[r0:r0 + ch, :]

    @pl.when(s == 0)
    def _():
        sb_ref[...] = sb0_ref[...]

    @pl.when(s == nb)
    def _():
        sf_ref[...] = sf0_ref[...]

    @pl.when(s < nb)
    def _():
        blk = nb - 1 - s
        zb = jnp.exp(lgb * ii)
        gch = jnp.exp(lgb * float(ch))
        for cc in reversed(range(nc)):
            r0 = cc * ch
            kc = rot(k_ref[r0:r0 + ch, :], r0) * scale
            sbs_ref[blk * nc + cc] = sb_ref[...]
            upd = _bdot((kc * zb).T.astype(BF16), v_ref[r0:r0 + ch, :])
            sb_ref[...] = sb_ref[...] * gch + upd

    @pl.when(s >= nb)
    def _():
        blk = s - nb
        ri = lax.broadcasted_iota(jnp.int32, (ch, ch), 0)
        ci = lax.broadcasted_iota(jnp.int32, (ch, ch), 1)
        rel = (ri - ci).astype(F32)
        dmat = jnp.where(ri > ci, jnp.exp(lgf * jnp.maximum(rel, 0.0)),
                         jnp.where(ri < ci, jnp.exp(lgb * jnp.maximum(-rel, 0.0)), 2.0))
        xf = jnp.exp(lgf * (ii + 1.0))
        xb = jnp.exp(lgb * (float(ch) - ii))
        zf = jnp.exp(lgf * (float(ch) - 1.0 - ii))
        gch = jnp.exp(lgf * float(ch))
        for cc in range(nc):
            r0 = cc * ch
            qc = rot(q_ref[r0:r0 + ch, :], r0)
            kc = rot(k_ref[r0:r0 + ch, :], r0) * scale
            vc = v_ref[r0:r0 + ch, :]
            sc = _bdot_nt(qc.astype(BF16), kc.astype(BF16)) * dmat
            qx = jnp.concatenate([qc * xf, qc * xb], axis=1).astype(BF16)
            st = jnp.concatenate([sf_ref[...], sbs_ref[blk * nc + cc]], axis=0).astype(BF16)
            o = _bdot(sc.astype(BF16), vc) + _bdot(qx, st)
            sf_ref[...] = sf_ref[...] * gch + _bdot((kc * zf).T.astype(BF16), vc)
            y = o * lax.rsqrt(jnp.mean(o * o, axis=-1, keepdims=True) + EPS)
            o_ref[r0:r0 + ch, :] = (y * _silu(g_ref[r0:r0 + ch, :].astype(F32))).astype(BF16)


def _retention(p, cosf, sinf, pm, lgf, lgb, sf0, sb0, *, q_off, k_off, v_off, g_off, heads, dk, dv, tb):
    b, l, _ = p.shape
    nb = l // tb
    fwd = lambda s: jnp.maximum(s - nb, 0)
    both = lambda s: jnp.where(s < nb, nb - 1 - s, s - nb)
    return pl.pallas_call(
        functools.partial(_ret_kernel, nb=nb, dk=dk),
        grid=(b, heads, 2 * nb),
        in_specs=[pl.BlockSpec((None, tb, dk), lambda i, h, s: (i, fwd(s), q_off // dk + h)),
                  pl.BlockSpec((None, tb, dk), lambda i, h, s: (i, both(s), k_off // dk + h)),
                  pl.BlockSpec((None, tb, dv), lambda i, h, s: (i, both(s), v_off // dv + h)),
                  pl.BlockSpec((None, tb, dv), lambda i, h, s: (i, fwd(s), g_off // dv + h)),
                  pl.BlockSpec((tb, dk), lambda i, h, s: (both(s), 0)),
                  pl.BlockSpec((tb, dk), lambda i, h, s: (both(s), 0)),
                  pl.BlockSpec((dk, dk), lambda i, h, s: (0, 0)),
                  pl.BlockSpec((None, 1, 1), lambda i, h, s: (h, 0, 0)),
                  pl.BlockSpec((None, 1, 1), lambda i, h, s: (h, 0, 0)),
                  pl.BlockSpec((None, None, dk, dv), lambda i, h, s: (i, h, 0, 0)),
                  pl.BlockSpec((None, None, dk, dv), lambda i, h, s: (i, h, 0, 0))],
        out_specs=pl.BlockSpec((None, tb, dv), lambda i, h, s: (i, fwd(s), h)),
        out_shape=jax.ShapeDtypeStruct((b, l, heads * dv), BF16),
        scratch_shapes=[pltpu.VMEM((dk, dv), F32), pltpu.VMEM((dk, dv), F32),
                        pltpu.VMEM((l // RET_CHUNK, dk, dv), F32)],
        compiler_params=_cparams("arbitrary", "arbitrary", "arbitrary"),
        name="ret",
    )(p, p, p, p, cosf, sinf, pm, lgf, lgb, sf0, sb0)


def _merge_kernel(y_ref, u_ref, x0_ref, r_ref, ah_ref, ar_ref, x_ref, g1_ref, sh_ref, sc_ref, n2_ref, hb_ref,
                  wh_ref, wr_ref, wo_ref, x2_ref, h2_ref):
    yh = (y_ref[...].astype(F32) + u_ref[...].astype(F32) * hb_ref[...]) * x0_ref[...].astype(F32)
    y_h = _bdot(yh.astype(BF16), wh_ref[...])
    y_r = _bdot(r_ref[...], wr_ref[...])
    m = _sigmoid(ah_ref[...].astype(F32)) * y_h + _sigmoid(ar_ref[...].astype(F32)) * y_r
    x2 = x_ref[...] + g1_ref[...] * _bdot(m.astype(BF16), wo_ref[...])
    x2_ref[...] = x2
    h2_ref[...] = (_rms(x2, n2_ref[...]) * (1.0 + sc_ref[...]) + sh_ref[...]).astype(BF16)


def _merge(y, u, x0, r, p, x, g1, sh2, sc2, norm2, hy_bias, wh, wr, wo, *, ah_off, ar_off, tm):
    b, l, d = x.shape
    c = y.shape[2]
    rw = r.shape[2]
    tile = lambda w, cb=0: pl.BlockSpec((None, tm, w), lambda i, j: (i, j, cb))
    vec = pl.BlockSpec((None, 1, d), lambda i, j: (i, 0, 0))
    return pl.pallas_call(
        _merge_kernel,
        grid=(b, l // tm),
        in_specs=[tile(c), tile(c), tile(c), tile(rw), tile(d, ah_off // d), tile(d, ar_off // d), tile(d),
                  vec, vec, vec,
                  pl.BlockSpec((1, d), lambda i, j: (0, 0)),
                  pl.BlockSpec((1, c), lambda i, j: (0, 0)),
                  _const_spec((c, d)), _const_spec((rw, d)), _const_spec((d, d))],
        out_specs=[tile(d), tile(d)],
        out_shape=[jax.ShapeDtypeStruct((b, l, d), F32), jax.ShapeDtypeStruct((b, l, d), BF16)],
        compiler_params=_cparams("arbitrary", "arbitrary"),
        name="merge",
    )(y, u, x0, r, p, p, x, g1, sh2, sc2, norm2, hy_bias, wh, wr, wo)


LOWEST_BITS = -0x00800001


def _rank_code(r):
    return np.array(LOWEST_BITS - r, np.int32).view(np.float32).item()


def _top_sorted(s, k):
    vals = []
    for r in range(k):
        m = jnp.max(s, axis=0, keepdims=True)
        vals.append(m)
        s = jnp.where(s == m, _rank_code(r), s)
    coded = s <= _rank_code(k - 1)
    rank = jnp.where(coded, LOWEST_BITS - pltpu.bitcast(s, jnp.int32), k).astype(F32)
    return vals, rank


def _pair_pack(v):
    b = pltpu.bitcast(v.astype(BF16).astype(F32), jnp.uint32)
    return b | (b >> 16)


def _row_bcast(row, rows):
    rep = pltpu.bitcast(jnp.broadcast_to(row, (8, row.shape[1])), BF16)
    return jnp.tile(rep, (rows // 16, 1))


def _peer_pairs(k):
    return [(i, j) for i in range(k) for j in range(k) if (i + 1) * (j + 1) <= k]


def _peer_kernel(h_ref, x_ref, g2_ref, fn_ref, wq_ref, kd_ref, u_ref, vt_ref, o_ref,
                 cn_ref, e1_ref, rk_ref, e2_ref, q_ref, s_ref, cand_ref, ht_ref, a_ref, w_ref, acc_ref,
                 *, topk, split):
    j = pl.program_id(2)
    nk = PEER_N_KEYS
    t = h_ref.shape[0]
    eb = u_ref.shape[0]
    pairs = _peer_pairs(topk)

    @pl.when(j == 0)
    def _():
        q = _bdot(h_ref[...], wq_ref[...]).astype(BF16)
        for hd in range(PEER_HEADS):
            q_ref[hd] = q[:, hd * PEER_DK:(hd + 1) * PEER_DK]
        cand_ref[...] = jnp.full(cand_ref.shape, NEG_INF, F32)

        def head(hd, s_ref, cand_ref):
            s_ref[...] = _bdot_nt(kd_ref[hd], q_ref[hd])
            for c0 in range(0, t, LANES):
                cols = slice(c0, c0 + LANES)
                s1, s2 = s_ref[0:nk, cols], s_ref[nk:, cols]
                a, rank1 = _top_sorted(s1, topk)
                b, rank2 = _top_sorted(s2, topk)
                for r, (pi, pj) in enumerate(pairs):
                    cand_ref[c0 // LANES, r:r + 1, :] = a[pi] + b[pj]
                cs = cand_ref[c0 // LANES]
                tau = _top_sorted(cs, topk)[0][-1]
                sel = cs >= tau
                mx = a[0] + b[0]
                z = jnp.sum(jnp.where(sel, jnp.exp(cs - mx), 0.0), axis=0, keepdims=True)
                self = jnp.where(sel, 1.0, 0.0)
                cnt1 = jnp.zeros(s1.shape, F32)
                for c in range(topk // 2):
                    height = sum(self[r:r + 1] for r, (_, pj) in enumerate(pairs) if pj == c)
                    cnt1 = jnp.where(rank1 < height, float(c + 1), cnt1)
                width0 = sum(self[r:r + 1] for r, (pi, _) in enumerate(pairs) if pi == 0)
                cnt1 = jnp.where(rank1 < 1.0, width0, cnt1)
                cn_ref[hd, :, cols] = _pair_pack(cnt1)
                e1_ref[hd, :, cols] = _pair_pack(jnp.exp(s1 - a[0]))
                rk_ref[hd, :, cols] = rank2.astype(BF16)
                e2_ref[hd, :, cols] = (jnp.exp(s2 - b[0]) / z).astype(BF16)

        def head_group(i, carry):
            for k in range(HEAD_UNROLL):
                head(i * HEAD_UNROLL + k, s_ref.at[k], cand_ref.at[k])
            return carry

        lax.fori_loop(0, PEER_HEADS // HEAD_UNROLL, head_group, 0)
        acc_ref[...] = jnp.zeros_like(acc_ref)
        ht_ref[...] = h_ref[...].astype(F32).T.astype(BF16)

    part = eb // split
    zero = jnp.zeros((), BF16)

    for s in range(split):
        rows = slice(s * part, (s + 1) * part)
        a_ref[rows, :] = _bdot(u_ref[rows, :], ht_ref[...]).astype(BF16)
    for s in range(split):
        rows = slice(s * part, (s + 1) * part)
        for g in range(part // nk):
            r0 = s * part + g * nk
            i = j * (eb // nk) + r0 // nk
            act = _gelu_tanh(a_ref[r0:r0 + nk, :])
            gate = None
            for hd in range(PEER_HEADS):
                cnt = _row_bcast(cn_ref[hd, pl.ds(i, 1), :], nk)
                e1 = _row_bcast(e1_ref[hd, pl.ds(i, 1), :], nk)
                term = jnp.where(rk_ref[hd] < cnt, e2_ref[hd], zero) * e1
                gate = term if gate is None else gate + term
            w_ref[r0:r0 + nk, :] = gate * act
        acc_ref[...] += _bdot(vt_ref[:, rows], w_ref[rows, :])

    @pl.when(j == pl.num_programs(2) - 1)
    def _():
        x3 = x_ref[...] + g2_ref[...] * acc_ref[...].T
        o_ref[...] = _rms(x3, fn_ref[...])


def _peer(h2, x2, g2, fnorm, wq, kd, u_bf, vt_bf, *, t, eb):
    b, l, d = x2.shape
    e = u_bf.shape[0]
    nk = PEER_N_KEYS
    npair = -(-len(_peer_pairs(PEER_TOPK)) // 8) * 8
    nj = e // eb
    tile =pl.BlockSpec((None, t, d), lambda i, m, j: (i, m, 0))
    return pl.pallas_call(
        functools.partial(_peer_kernel, topk=PEER_TOPK, split=4),
        grid=(b, l // t, nj),
        in_specs=[tile, tile,
                  pl.BlockSpec((None, 1, d), lambda i, m, j: (i, 0, 0)),
                  pl.BlockSpec((1, d), lambda i, m, j: (0, 0)),
                  _const_spec((d, PEER_HEADS * PEER_DK)),
                  _const_spec((PEER_HEADS, 2 * nk, PEER_DK)),
                  pl.BlockSpec((eb, d), lambda i, m, j: (j, 0)),
                  pl.BlockSpec((d, eb), lambda i, m, j: (0, j))],
        out_specs=tile,
        out_shape=jax.ShapeDtypeStruct((b, l, d), F32),
        scratch_shapes=[pltpu.VMEM((PEER_HEADS, nk, t), jnp.uint32), pltpu.VMEM((PEER_HEADS, nk, t), jnp.uint32),
                        pltpu.VMEM((PEER_HEADS, nk, t), BF16), pltpu.VMEM((PEER_HEADS, nk, t), BF16),
                        pltpu.VMEM((PEER_HEADS, t, PEER_DK), BF16),
                          pltpu.VMEM((HEAD_UNROLL, 2 * nk, t), F32),
                          pltpu.VMEM((HEAD_UNROLL, t // LANES, npair, LANES), F32),
                          pltpu.VMEM((d, t), BF16),
                          pltpu.VMEM((eb, t), BF16), pltpu.VMEM((eb, t), BF16),
                          pltpu.VMEM((d, t), F32)],
        compiler_params=_cparams("arbitrary", "arbitrary", "arbitrary"),
        name="peer",
    )(h2, x2, g2, fnorm, wq, kd, u_bf, vt_bf)


def _filter_features(l, emb):
    bands_n = (emb - 1) // 2
    t = jnp.linspace(0.0, 1.0, l, dtype=F32)[:, None]
    bands = jnp.linspace(1e-4, bands_n - 1, bands_n, dtype=F32)[None, :]
    w = (2.0 * math.pi / l) * jnp.arange(l, dtype=F32)[:, None]
    z = jnp.concatenate([t, jnp.cos(bands * w), -jnp.sin(bands * w)], axis=-1)
    return t, z


def _rotary_tables(l, dk):
    half = dk // 2
    nf = half // 2
    inv = ROPE_BASE ** (-jnp.arange(nf, dtype=F32) / nf)
    pos = jnp.arange(l, dtype=jnp.int32)
    rows = (pos // GRID_W).astype(F32)
    cols = (pos % GRID_W).astype(F32)
    ar = rows[:, None] * inv[None, :]
    ac = cols[:, None] * inv[None, :]
    cosf = jnp.concatenate([jnp.cos(ar), jnp.cos(ar), jnp.cos(ac), jnp.cos(ac)], axis=1)
    sinf = jnp.concatenate([-jnp.sin(ar), jnp.sin(ar), -jnp.sin(ac), jnp.sin(ac)], axis=1)
    lane = jnp.arange(dk, dtype=jnp.int32)
    pm = (lane[:, None] == (lane[None, :] ^ nf)).astype(BF16)
    return cosf, sinf, pm


def kernel(x, c, ctx, c_ctx, w_ada, b_ada, norm1, norm2, w_in, hy_conv_w, hy_conv_b, hy_fw1, hy_fb1, hy_fw2, hy_fb2, hy_fw3, hy_fb3, hy_fw4, hy_sin_freq, hy_deltas, hy_bias, ret_log_decay_f, ret_log_decay_b, w_hy_out, w_ret_out, w_o, peer_w_query, peer_sub_keys, peer_u, peer_v, final_norm):
    assert w_ada.shape[0] == 1, "single-layer configuration"
    b, l, d = x.shape
    assert b % 2 == 0 and l % RET_CHUNK == 0
    heads = RET_HEADS
    dk = d // 8
    dv = 2 * dk
    c_hy = d
    hy_cols = 3 * c_hy
    q_off = hy_cols
    k_off = q_off + heads * dk
    v_off = k_off + heads * dk
    g_off = v_off + heads * dv
    ah_off = g_off + heads * dv
    ar_off = ah_off + d

    rows = -(-(b + 1) // 8) * 8
    cc = jnp.zeros((rows, d), F32).at[:b].set(c).at[b].set(c_ctx)
    mod = _mod(cc, w_ada[0], b_ada[0][None, :])
    mod_l = mod[:b].reshape(b, 6, 1, d)
    sh1, sc1, g1, sh2, sc2, g2 = (mod_l[:, i] for i in range(6))
    mod_c = mod[b].reshape(6, 1, d)
    csh1, csc1 = mod_c[0], mod_c[1]

    w_in_bf = w_in[0].astype(BF16)
    n1g = norm1[0][None, :]
    lgf = ret_log_decay_f[0]
    lgb = ret_log_decay_b[0]

    sf0, sb0 = _ctx_states(ctx, csh1, csc1, n1g, w_in_bf, lgf[:, None], lgb[:, None],
                           k_off=k_off, v_off=v_off, heads=heads, dk=dk, dv=dv)

    p = _inproj(x, sh1, sc1, n1g, w_in_bf, tm=min(512, l))

    u, x0c = _hy_pre(p, hy_conv_w[0], hy_conv_b[0][None, :], c=c_hy, tl=min(1024, l))
    emb = hy_fw1.shape[1]
    t_lin, z = _filter_features(l, emb)
    epad = -(-emb // 64) * 64
    z = jnp.pad(z, ((0, 0), (0, epad - emb)))
    fw1 = jnp.pad(hy_fw1[0], ((0, epad - emb), (0, 0)))
    hf, hb, asum = _hy_filter(z, t_lin, fw1, hy_fb1[0][None, :], hy_fw2[0], hy_fb2[0][None, :],
                              hy_fw3[0], hy_fb3[0][None, :], hy_fw4[0], hy_sin_freq[0][None, :],
                              hy_deltas[0][None, :], tl=min(1024, l))
    y = _hyena_long_conv(u, hf, hb, asum)

    cosf, sinf, pm = _rotary_tables(l, dk)
    r = _retention(p, cosf, sinf, pm, lgf[:, None, None], lgb[:, None, None], sf0, sb0,
                   q_off=q_off, k_off=k_off, v_off=v_off, g_off=g_off, heads=heads, dk=dk, dv=dv,
                   tb=min(4096, l))

    x2, h2 = _merge(y, u, x0c, r, p, x, g1, sh2, sc2, norm2[0][None, :], hy_bias[0][None, :],
                    w_hy_out[0].astype(BF16), w_ret_out[0].astype(BF16), w_o[0].astype(BF16),
                    ah_off=ah_off, ar_off=ar_off, tm=min(512, l))

    sk = peer_sub_keys[0]
    zk = jnp.zeros_like(sk[:, 0])
    kd = jnp.concatenate([jnp.concatenate([sk[:, 0], zk], axis=2),
                          jnp.concatenate([zk, sk[:, 1]], axis=2)], axis=1).astype(BF16)
    out = _peer(h2, x2, g2, final_norm[None, :], peer_w_query[0].astype(BF16), kd,
                peer_u[0].astype(BF16), peer_v[0].T.astype(BF16), t=min(512, l), eb=2048)
    return out
```
